```python
import math
import jax, jax.numpy as jnp
from jax import lax
import numpy as np

D_MODEL = 1024
BATCH = 1
SEQ = 16384
DEPTH = 2
DEC_BATCH = 2
DEC_SEQ = 16384
PAST_LEN = 128

HEAD_DIM = 64
H_A = 8
D_A = H_A * HEAD_DIM
H_B = 8
H_KV = 2
GROUP = H_B // H_KV
D_B = H_B * HEAD_DIM
D_KV = H_KV * HEAD_DIM
D_MIX = D_A + D_B
W_IN_COLS = 3 * D_A + D_B + 2 * D_KV
GRID_W = 64
KH_MAX = 8
KW = 16
WINDOW = 128
BLOCK = 128
D_FF = 2816
CONV_W = 3
EPS = 1e-6

kernel_name = "hymba_natten_swa_convffn_encoder"


def rmsnorm(x, g):
    xf = x.astype(jnp.float32)
    inv = lax.rsqrt(jnp.mean(xf * xf, axis=-1, keepdims=True) + EPS)
    return (xf * inv).astype(x.dtype) * g


def neighbourhood_attention(q, k, v, rpb):
    B, T = q.shape[0], q.shape[1]
    rows = T // GRID_W
    kh = min(KH_MAX, rows)
    scale = 1.0 / math.sqrt(HEAD_DIM)
    q = q.reshape(B, rows, GRID_W, H_A, HEAD_DIM)
    k = k.reshape(B, rows, GRID_W, H_A, HEAD_DIM)
    v = v.reshape(B, rows, GRID_W, H_A, HEAD_DIM)
    cols = np.arange(GRID_W)
    col_start = np.clip(cols - KW // 2, 0, GRID_W - KW)
    col_idx = col_start[:, None] + np.arange(KW)
    dc = jnp.asarray(col_idx - cols[:, None] + (KW - 1))

    def row_step(r):
        rs = jnp.clip(r - kh // 2, 0, rows - kh)
        q_r = lax.dynamic_index_in_dim(q, r, axis=1, keepdims=False)
        k_w = lax.dynamic_slice_in_dim(k, rs, kh, axis=1)[:, :, col_idx]
        v_w = lax.dynamic_slice_in_dim(v, rs, kh, axis=1)[:, :, col_idx]
        dr = rs + jnp.arange(kh) - r + (KH_MAX - 1)
        bias = rpb[:, dr[:, None, None], dc[None]]
        bias = bias.transpose(0, 2, 1, 3).astype(jnp.float32)
        s = jnp.einsum('bqhd,brqwhd->bhqrw', q_r, k_w).astype(jnp.float32) * scale + bias[None]
        p = jax.nn.softmax(s.reshape(B, H_A, GRID_W, kh * KW), axis=-1)
        p = p.reshape(B, H_A, GRID_W, kh, KW).astype(v.dtype)
        return jnp.einsum('bhqrw,brqwhd->bqhd', p, v_w)

    out = lax.map(row_step, jnp.arange(rows))
    return out.transpose(1, 0, 2, 3, 4).reshape(B, T, D_A)


def sliding_window_attention(q, k, v, sinks):
    B, T = q.shape[0], q.shape[1]
    nb = T // BLOCK
    scale = 1.0 / math.sqrt(HEAD_DIM)
    slopes = jnp.exp2(-8.0 * jnp.arange(1, H_B + 1, dtype=jnp.float32) / H_B)
    slope = slopes.reshape(H_KV, GROUP)[None, :, :, None, None]
    sink = sinks.astype(jnp.float32).reshape(H_KV, GROUP)[None, :, :, None]
    q = q.reshape(B, nb, BLOCK, H_KV, GROUP, HEAD_DIM)
    pad = ((0, 0), (BLOCK, BLOCK), (0, 0), (0, 0))
    kp = jnp.pad(k, pad)
    vp = jnp.pad(v, pad)

    def block_step(i):
        q_i = lax.dynamic_index_in_dim(q, i, axis=1, keepdims=False)
        k_i = lax.dynamic_slice_in_dim(kp, i * BLOCK, 3 * BLOCK, axis=1)
        v_i = lax.dynamic_slice_in_dim(vp, i * BLOCK, 3 * BLOCK, axis=1)
        t = i * BLOCK + jnp.arange(BLOCK)
        s_pos = (i - 1) * BLOCK + jnp.arange(3 * BLOCK)
        dist = jnp.abs(t[:, None] - s_pos[None, :])
        valid = (dist <= WINDOW) & (s_pos >= 0)[None, :] & (s_pos < T)[None, :]
        logits = jnp.einsum('bqkgd,bskd->bkgqs', q_i, k_i).astype(jnp.float32) * scale
        logits = logits - slope * dist.astype(jnp.float32)
        logits = jnp.where(valid, logits, -jnp.inf)
        m = jnp.maximum(jnp.max(logits, axis=-1), sink)
        e = jnp.exp(logits - m[..., None])
        denom = jnp.sum(e, axis=-1) + jnp.exp(sink - m)
        p = (e / denom[..., None]).astype(v.dtype)
        return jnp.einsum('bkgqs,bskd->bqkgd', p, v_i).reshape(B, BLOCK, D_B)

    out = lax.map(block_step, jnp.arange(nb))
    return out.transpose(1, 0, 2, 3).reshape(B, T, D_B)


def dwconv3(u, w, b):
    up = jnp.pad(u, ((0, 0), (1, 1), (0, 0)))
    return up[:, :-2] * w[0] + up[:, 1:-1] * w[1] + up[:, 2:] * w[2] + b


def trunk(x, norm_mix, w_in, rpb, sinks, norm_grp, w_out, norm_ffn, w_up, conv_w, conv_b, w_down, norm_final):
    B, T = x.shape[0], x.shape[1]
    for l in range(DEPTH):
        h = rmsnorm(x, norm_mix[l])
        proj = h @ w_in[l]
        o = 0
        qa = proj[..., o:o + D_A].reshape(B, T, H_A, HEAD_DIM); o += D_A
        ka = proj[..., o:o + D_A].reshape(B, T, H_A, HEAD_DIM); o += D_A
        va = proj[..., o:o + D_A].reshape(B, T, H_A, HEAD_DIM); o += D_A
        qb = proj[..., o:o + D_B].reshape(B, T, H_B, HEAD_DIM); o += D_B
        kb = proj[..., o:o + D_KV].reshape(B, T, H_KV, HEAD_DIM); o += D_KV
        vb = proj[..., o:o + D_KV].reshape(B, T, H_KV, HEAD_DIM)
        out_a = neighbourhood_attention(qa, ka, va, rpb[l])
        out_b = sliding_window_attention(qb, kb, vb, sinks[l])
        out_a = rmsnorm(out_a, norm_grp[l, :D_A])
        out_b = rmsnorm(out_b, norm_grp[l, D_A:])
        x = x + jnp.concatenate([out_a, out_b], axis=-1) @ w_out[l]
        h = rmsnorm(x, norm_ffn[l])
        u = dwconv3(h @ w_up[l], conv_w[l], conv_b[l])
        gate, val = u[..., :D_FF], u[..., D_FF:]
        x = x + (jax.nn.silu(gate) * val) @ w_down[l]
    return rmsnorm(x, norm_final)


def setup_inputs(seed: int = 0) -> dict:
    key = jax.random.key(seed)
    ks = jax.random.split(key, 16)
    f32 = jnp.float32
    nrm = lambda k, shape, s: jax.random.normal(k, shape, f32) * s
    return {
        "x_prompt": nrm(ks[0], (BATCH, SEQ, D_MODEL), 1.0),
        "x_sample": nrm(ks[1], (DEC_BATCH, DEC_SEQ, D_MODEL), 1.0),
        "norm_mix": 1.0 + nrm(ks[2], (DEPTH, D_MODEL), 0.02),
        "w_in": nrm(ks[3], (DEPTH, D_MODEL, W_IN_COLS), D_MODEL ** -0.5),
        "rpb": nrm(ks[4], (DEPTH, H_A, 2 * KH_MAX - 1, 2 * KW - 1), 0.5),
        "sinks": nrm(ks[5], (DEPTH, H_B), 0.5),
        "norm_grp": 1.0 + nrm(ks[6], (DEPTH, D_MIX), 0.02),
        "w_out": nrm(ks[7], (DEPTH, D_MIX, D_MODEL), D_MIX ** -0.5),
        "norm_ffn": 1.0 + nrm(ks[8], (DEPTH, D_MODEL), 0.02),
        "w_up": nrm(ks[9], (DEPTH, D_MODEL, 2 * D_FF), D_MODEL ** -0.5),
        "conv_w": nrm(ks[10], (DEPTH, CONV_W, 2 * D_FF), CONV_W ** -0.5),
        "conv_b": nrm(ks[11], (DEPTH, 2 * D_FF), 0.02),
        "w_down": nrm(ks[12], (DEPTH, D_FF, D_MODEL), D_FF ** -0.5),
        "norm_final": 1.0 + nrm(ks[13], (D_MODEL,), 0.02),
    }


def reference(x_prompt, x_sample, norm_mix, w_in, rpb, sinks, norm_grp, w_out, norm_ffn, w_up, conv_w, conv_b, w_down, norm_final):
    y_prompt = trunk(x_prompt, norm_mix, w_in, rpb, sinks, norm_grp, w_out, norm_ffn, w_up, conv_w, conv_b, w_down, norm_final)
    y_sample = trunk(x_sample, norm_mix, w_in, rpb, sinks, norm_grp, w_out, norm_ffn, w_up, conv_w, conv_b, w_down, norm_final)
    return (y_prompt, y_sample)
```

```python
import functools
import math

import jax
import jax.numpy as jnp
import numpy as np
from jax import lax
from jax.experimental import pallas as pl
from jax.experimental.pallas import tpu as pltpu

D_MODEL = 1024
HEAD_DIM = 64
H_A = 8
D_A = H_A * HEAD_DIM
H_B = 8
H_KV = 2
GROUP = H_B // H_KV
D_B = H_B * HEAD_DIM
D_KV = H_KV * HEAD_DIM
W_IN_COLS = 3 * D_A + D_B + 2 * D_KV
GRID_W = 64
KH = 8
KW = 16
WINDOW = 128
BLOCK = 128
D_FF = 2816
EPS = 1e-6
SCALE = 1.0 / math.sqrt(HEAD_DIM)

F32 = jnp.float32
BF16 = jnp.bfloat16

VMEM_LIMIT_BYTES = 56 * 1024 * 1024

TM_PROJ = 512
TM_FFN = 512
HALO = 16
FF_CHUNK = 256
NA_ROWS = 8


def _rms(x, g):
    inv = lax.rsqrt(jnp.mean(x * x, axis=-1, keepdims=True) + EPS)
    return (x * inv) * g


def _params(n_axes):
    return pltpu.CompilerParams(
        dimension_semantics=("parallel",) * n_axes,
        vmem_limit_bytes=VMEM_LIMIT_BYTES,
    )


_SEGS = (
    (0, D_A, SCALE),
    (D_A, D_A, None),
    (2 * D_A, D_A, None),
    (3 * D_A, D_B, SCALE),
    (3 * D_A + D_B, D_KV, None),
    (3 * D_A + D_B + D_KV, D_KV, None),
)


def _in_proj_kernel(x_ref, g_ref, w_ref, *out_refs):
    h = _rms(x_ref[...], g_ref[...]).astype(BF16)
    for (start, width, scale), o_ref in zip(_SEGS, out_refs):
        p = jnp.dot(h, w_ref[:, start:start + width], preferred_element_type=F32)
        if scale is not None:
            p = p * scale
        o_ref[...] = p.astype(BF16)


def _in_proj(x, g, w):
    m = x.shape[0]
    tm = TM_PROJ
    out_shape = tuple(jax.ShapeDtypeStruct((m, width), BF16) for _, width, _ in _SEGS)
    out_specs = tuple(pl.BlockSpec((tm, width), lambda i: (i, 0)) for _, width, _ in _SEGS)
    return pl.pallas_call(
        _in_proj_kernel,
        grid=(m // tm,),
        in_specs=[
            pl.BlockSpec((tm, D_MODEL), lambda i: (i, 0)),
            pl.BlockSpec((1, D_MODEL), lambda i: (0, 0)),
            pl.BlockSpec((D_MODEL, W_IN_COLS), lambda i: (0, 0)),
        ],
        out_specs=out_specs,
        out_shape=out_shape,
        compiler_params=_params(1),
        name="in_proj",
    )(x, g, w)


def _na_bias_table(rpb):
    cols = np.arange(GRID_W)
    col_start = np.clip(cols - KW // 2, 0, GRID_W - KW)
    c2 = cols[None, :]
    valid = (c2 >= col_start[:, None]) & (c2 < col_start[:, None] + KW)
    dc = np.clip(c2 - cols[:, None] + (KW - 1), 0, 2 * KW - 2)
    e = jnp.where(jnp.asarray(valid)[None, None], rpb[:, :, dc].astype(F32), -jnp.inf)
    tabs = []
    for d in range(KH):
        t = e[:, KH - 1 - d:2 * KH - 1 - d]
        tabs.append(t.transpose(0, 2, 1, 3).reshape(H_A, GRID_W, KH * GRID_W))
    return jnp.stack(tabs)


def _na_kernel(q_ref, kp_ref, kc_ref, kn_ref, vp_ref, vc_ref, vn_ref, bias_ref, g_ref,
               o_ref, kwin, vwin, obuf, *, rows):
    i = pl.program_id(1)
    blk = NA_ROWS * GRID_W
    kwin[0:blk] = kp_ref[...]
    kwin[blk:2 * blk] = kc_ref[...]
    kwin[2 * blk:3 * blk] = kn_ref[...]
    vwin[0:blk] = vp_ref[...]
    vwin[blk:2 * blk] = vc_ref[...]
    vwin[2 * blk:3 * blk] = vn_ref[...]
    g = g_ref[...]

    def row_body(rr, carry):
        r = i * NA_ROWS + rr
        rs = jnp.clip(r - KH // 2, 0, rows - KH)
        d = r - rs
        off = pl.multiple_of((rs - (i - 1) * NA_ROWS) * GRID_W, GRID_W)
        qoff = pl.multiple_of(rr * GRID_W, GRID_W)
        for h in range(H_A):
            lanes = slice(h * HEAD_DIM, (h + 1) * HEAD_DIM)
            q = q_ref[pl.ds(qoff, GRID_W), lanes]
            k = kwin[pl.ds(off, KH * GRID_W), lanes]
            v = vwin[pl.ds(off, KH * GRID_W), lanes]
            s = lax.dot_general(q, k, (((1,), (1,)), ((), ())), preferred_element_type=F32)
            s = s + bias_ref[d, h]
            m = jnp.max(s, axis=-1, keepdims=True)
            p = jnp.exp(s - m)
            l = jnp.sum(p, axis=-1, keepdims=True)
            o = jnp.dot(p.astype(BF16), v, preferred_element_type=F32)
            obuf[:, lanes] = o / l
        o_ref[pl.ds(qoff, GRID_W), :] = _rms(obuf[...], g).astype(BF16)
        return carry

    lax.fori_loop(0, NA_ROWS, row_body, 0)


def _na_attention(q, k, v, bias, g, batch, seq):
    rows = seq // GRID_W
    nblk = rows // NA_ROWS
    blk = NA_ROWS * GRID_W
    cur = lambda b, i: (b * nblk + i, 0)
    prev = lambda b, i: (b * nblk + jnp.maximum(i - 1, 0), 0)
    nxt = lambda b, i: (b * nblk + jnp.minimum(i + 1, nblk - 1), 0)
    spec = lambda f: pl.BlockSpec((blk, D_A), f)
    return pl.pallas_call(
        functools.partial(_na_kernel, rows=rows),
        grid=(batch, nblk),
        in_specs=[
            spec(cur), spec(prev), spec(cur), spec(nxt), spec(prev), spec(cur), spec(nxt),
            pl.BlockSpec((KH, H_A, GRID_W, KH * GRID_W), lambda b, i: (0, 0, 0, 0)),
            pl.BlockSpec((1, D_A), lambda b, i: (0, 0)),
        ],
        out_specs=spec(cur),
        out_shape=jax.ShapeDtypeStruct((batch * seq, D_A), BF16),
        scratch_shapes=[
            pltpu.VMEM((3 * blk, D_A), BF16),
            pltpu.VMEM((3 * blk, D_A), BF16),
            pltpu.VMEM((GRID_W, D_A), F32),
        ],
        compiler_params=_params(2),
        name="na_attention",
    )(q, k, k, k, v, v, v, bias, g)


def _swa_kernel(sink_ref, q_ref, kp_ref, kc_ref, kn_ref, vp_ref, vc_ref, vn_ref, g_ref,
                o_ref, kwin, vwin, obuf, *, nb):
    i = pl.program_id(1)
    kwin[0:BLOCK] = kp_ref[...]
    kwin[BLOCK:2 * BLOCK] = kc_ref[...]
    kwin[2 * BLOCK:3 * BLOCK] = kn_ref[...]
    vwin[0:BLOCK] = vp_ref[...]
    vwin[BLOCK:2 * BLOCK] = vc_ref[...]
    vwin[2 * BLOCK:3 * BLOCK] = vn_ref[...]

    tq = lax.broadcasted_iota(jnp.int32, (BLOCK, 3 * BLOCK), 0)
    ts = lax.broadcasted_iota(jnp.int32, (BLOCK, 3 * BLOCK), 1)
    dist = jnp.abs(tq - (ts - BLOCK))
    valid = dist <= WINDOW
    valid = valid & ((ts >= BLOCK) | (i > 0)) & ((ts < 2 * BLOCK) | (i < nb - 1))
    neg_dist = jnp.where(valid, -dist.astype(F32), -jnp.inf)

    for kh in range(H_KV):
        kv_lanes = slice(kh * HEAD_DIM, (kh + 1) * HEAD_DIM)
        k = kwin[:, kv_lanes]
        v = vwin[:, kv_lanes]
        for gq in range(GROUP):
            h = kh * GROUP + gq
            lanes = slice(h * HEAD_DIM, (h + 1) * HEAD_DIM)
            slope = 2.0 ** (-8.0 * (h + 1) / H_B)
            sink = sink_ref[h]
            s = lax.dot_general(q_ref[:, lanes], k, (((1,), (1,)), ((), ())),
                                preferred_element_type=F32)
            logits = s + slope * neg_dist
            m = jnp.maximum(jnp.max(logits, axis=-1, keepdims=True), sink)
            e = jnp.exp(logits - m)
            denom = jnp.sum(e, axis=-1, keepdims=True) + jnp.exp(sink - m)
            o = jnp.dot(e.astype(BF16), v, preferred_element_type=F32)
            obuf[:, lanes] = o / denom
    o_ref[...] = _rms(obuf[...], g_ref[...]).astype(BF16)


def _swa_attention(q, k, v, sinks, g, batch, seq):
    nb = seq // BLOCK
    cur = lambda b, i: (b * nb + i, 0)
    prev = lambda b, i: (b * nb + jnp.maximum(i - 1, 0), 0)
    nxt = lambda b, i: (b * nb + jnp.minimum(i + 1, nb - 1), 0)
    kv = lambda f: pl.BlockSpec((BLOCK, D_KV), f)
    return pl.pallas_call(
        functools.partial(_swa_kernel, nb=nb),
        grid=(batch, nb),
        in_specs=[
            pl.BlockSpec(memory_space=pltpu.SMEM),
            pl.BlockSpec((BLOCK, D_B), cur),
            kv(prev), kv(cur), kv(nxt), kv(prev), kv(cur), kv(nxt),
            pl.BlockSpec((1, D_B), lambda b, i: (0, 0)),
        ],
        out_specs=pl.BlockSpec((BLOCK, D_B), cur),
        out_shape=jax.ShapeDtypeStruct((batch * seq, D_B), BF16),
        scratch_shapes=[
            pltpu.VMEM((3 * BLOCK, D_KV), BF16),
            pltpu.VMEM((3 * BLOCK, D_KV), BF16),
            pltpu.VMEM((BLOCK, D_B), F32),
        ],
        compiler_params=_params(2),
        name="swa_attention",
    )(sinks, q, k, k, k, v, v, v, g)


def _out_proj_kernel(x_ref, a_ref, b_ref, w_ref, o_ref):
    acc = jnp.dot(a_ref[...], w_ref[0:D_A, :], preferred_element_type=F32)
    acc = acc + jnp.dot(b_ref[...], w_ref[D_A:, :], preferred_element_type=F32)
    o_ref[...] = x_ref[...] + acc


def _out_proj(x, a, b, w):
    m = x.shape[0]
    tm = TM_PROJ
    return pl.pallas_call(
        _out_proj_kernel,
        grid=(m // tm,),
        in_specs=[
            pl.BlockSpec((tm, D_MODEL), lambda i: (i, 0)),
            pl.BlockSpec((tm, D_A), lambda i: (i, 0)),
            pl.BlockSpec((tm, D_B), lambda i: (i, 0)),
            pl.BlockSpec((D_A + D_B, D_MODEL), lambda i: (0, 0)),
        ],
        out_specs=pl.BlockSpec((tm, D_MODEL), lambda i: (i, 0)),
        out_shape=jax.ShapeDtypeStruct((m, D_MODEL), F32),
        compiler_params=_params(1),
        name="out_proj",
    )(x, a, b, w)


def _ffn_kernel(xp_ref, x_ref, xn_ref, g_ref, wup_ref, cw_ref, cb_ref, wd_ref, gf_ref,
                o_ref, h_ext, act, *, tiles_per_seq, final):
    i = pl.program_id(0)
    tm = x_ref.shape[0]
    ext = tm + 2 * HALO
    g = g_ref[...]
    t = i % tiles_per_seq
    keep_prev = (t > 0).astype(F32)
    keep_next = (t < tiles_per_seq - 1).astype(F32)
    h_ext[0:HALO] = (_rms(xp_ref[...], g) * keep_prev).astype(BF16)
    h_ext[HALO:HALO + tm] = _rms(x_ref[...], g).astype(BF16)
    h_ext[HALO + tm:ext] = (_rms(xn_ref[...], g) * keep_next).astype(BF16)

    def conv(u, col0):
        cols = slice(col0, col0 + FF_CHUNK)
        up = pltpu.roll(u, 1, 0)[HALO:HALO + tm]
        un = pltpu.roll(u, ext - 1, 0)[HALO:HALO + tm]
        uc = u[HALO:HALO + tm]
        return (up * cw_ref[0:1, cols] + uc * cw_ref[1:2, cols] + un * cw_ref[2:3, cols]
                + cb_ref[0:1, cols])

    for c in range(D_FF // FF_CHUNK):
        gc0 = c * FF_CHUNK
        vc0 = D_FF + c * FF_CHUNK
        hx = h_ext[...]
        ug = jnp.dot(hx, wup_ref[:, gc0:gc0 + FF_CHUNK], preferred_element_type=F32)
        uv = jnp.dot(hx, wup_ref[:, vc0:vc0 + FF_CHUNK], preferred_element_type=F32)
        gate = conv(ug, gc0)
        val = conv(uv, vc0)
        act[:, gc0:gc0 + FF_CHUNK] = (gate * (1.0 / (1.0 + jnp.exp(-gate))) * val).astype(BF16)

    y = x_ref[...] + jnp.dot(act[...], wd_ref[...], preferred_element_type=F32)
    if final:
        y = _rms(y, gf_ref[...])
    o_ref[...] = y


def _ffn(x, g, w_up, conv_w, conv_b, w_down, g_final, seq, final):
    m = x.shape[0]
    tm = TM_FFN
    tiles_per_seq = seq // tm
    hb = tm // HALO
    n_halo_blocks = m // HALO
    const = lambda i: (0, 0)
    resident = dict(pipeline_mode=pl.Buffered(1))
    return pl.pallas_call(
        functools.partial(_ffn_kernel, tiles_per_seq=tiles_per_seq, final=final),
        grid=(m // tm,),
        in_specs=[
            pl.BlockSpec((HALO, D_MODEL), lambda i: (jnp.maximum(i * hb - 1, 0), 0)),
            pl.BlockSpec((tm, D_MODEL), lambda i: (i, 0)),
            pl.BlockSpec((HALO, D_MODEL), lambda i: (jnp.minimum((i + 1) * hb, n_halo_blocks - 1), 0)),
            pl.BlockSpec((1, D_MODEL), const),
            pl.BlockSpec((D_MODEL, 2 * D_FF), const, **resident),
            pl.BlockSpec((3, 2 * D_FF), const),
            pl.BlockSpec((1, 2 * D_FF), const),
            pl.BlockSpec((D_FF, D_MODEL), const, **resident),
            pl.BlockSpec((1, D_MODEL), const),
        ],
        out_specs=pl.BlockSpec((tm, D_MODEL), lambda i: (i, 0)),
        out_shape=jax.ShapeDtypeStruct((m, D_MODEL), F32),
        scratch_shapes=[
            pltpu.VMEM((tm + 2 * HALO, D_MODEL), BF16),
            pltpu.VMEM((tm, D_FF), BF16),
        ],
        compiler_params=_params(1),
        name="ffn",
    )(x, x, x, g, w_up, conv_w, conv_b, w_down, g_final)


def _trunk(x, p):
    batch, seq, _ = x.shape
    depth = p["w_in"].shape[0]
    xf = x.reshape(batch * seq, D_MODEL)
    for l in range(depth):
        qa, ka, va, qb, kb, vb = _in_proj(xf, p["norm_mix"][l], p["w_in"][l])
        oa = _na_attention(qa, ka, va, p["na_bias"][l], p["norm_grp_a"][l], batch, seq)
        ob = _swa_attention(qb, kb, vb, p["sinks"][l], p["norm_grp_b"][l], batch, seq)
        xf = _out_proj(xf, oa, ob, p["w_out"][l])
        xf = _ffn(xf, p["norm_ffn"][l], p["w_up"][l], p["conv_w"][l], p["conv_b"][l],
                  p["w_down"][l], p["norm_final"], seq, final=(l == depth - 1))
    return xf.reshape(batch, seq, D_MODEL)


def kernel(x_prompt, x_sample, norm_mix, w_in, rpb, sinks, norm_grp, w_out, norm_ffn, w_up,
           conv_w, conv_b, w_down, norm_final):
    depth = w_in.shape[0]
    p = {
        "norm_mix": norm_mix.reshape(depth, 1, D_MODEL),
        "w_in": w_in.astype(BF16),
        "na_bias": jnp.stack([_na_bias_table(rpb[l]) for l in range(depth)]),
        "sinks": sinks.astype(F32),
        "norm_grp_a": norm_grp[:, :D_A].reshape(depth, 1, D_A),
        "norm_grp_b": norm_grp[:, D_A:].reshape(depth, 1, D_B),
        "w_out": w_out.astype(BF16),
        "norm_ffn": norm_ffn.reshape(depth, 1, D_MODEL),
        "w_up": w_up.astype(BF16),
        "conv_w": conv_w,
        "conv_b": conv_b.reshape(depth, 1, 2 * D_FF),
        "w_down": w_down.astype(BF16),
        "norm_final": norm_final.reshape(1, D_MODEL),
    }
    return (_trunk(x_prompt, p), _trunk(x_sample, p))
```

```python
import functools
import math

import jax
import jax.numpy as jnp
import numpy as np
from jax import lax
from jax.experimental import pallas as pl
from jax.experimental.pallas import tpu as pltpu

D_MODEL = 1024
HEAD_DIM = 64
H_A = 8
D_A = H_A * HEAD_DIM
H_B = 8
H_KV = 2
GROUP = H_B // H_KV
D_B = H_B * HEAD_DIM
D_KV = H_KV * HEAD_DIM
W_IN_COLS = 3 * D_A + D_B + 2 * D_KV
GRID_W = 64
KH = 8
KW = 16
WINDOW = 128
BLOCK = 128
D_FF = 2816
EPS = 1e-6
SCALE = 1.0 / math.sqrt(HEAD_DIM)
LOG2E = math.log2(math.e)
Q_SCALE = SCALE * LOG2E

F32 = jnp.float32
BF16 = jnp.bfloat16

VMEM_LIMIT_BYTES = 56 * 1024 * 1024

TM_PROJ = 512
TM_FFN = 512
HALO = 16
FF_CHUNK = 256
NA_ROWS = 8
NA_HEADS_PER_GROUP = 4
SWA_BLOCKS = 2


def _rms(x, g):
    inv = lax.rsqrt(jnp.mean(x * x, axis=-1, keepdims=True) + EPS)
    return (x * inv) * g


def _params(n_axes):
    return pltpu.CompilerParams(
        dimension_semantics=("parallel",) * n_axes,
        vmem_limit_bytes=VMEM_LIMIT_BYTES,
    )


_SEGS = (
    (0, D_A, Q_SCALE),
    (D_A, D_A, None),
    (2 * D_A, D_A, None),
    (3 * D_A, D_B, Q_SCALE),
    (3 * D_A + D_B, D_KV, None),
    (3 * D_A + D_B + D_KV, D_KV, None),
)


def _in_proj_kernel(x_ref, g_ref, w_ref, *out_refs):
    h = _rms(x_ref[...], g_ref[...]).astype(BF16)
    for (start, width, scale), o_ref in zip(_SEGS, out_refs):
        p = jnp.dot(h, w_ref[:, start:start + width], preferred_element_type=F32)
        if scale is not None:
            p = p * scale
        o_ref[...] = p.astype(BF16)


def _in_proj(x, g, w):
    m = x.shape[0]
    tm = TM_PROJ
    out_shape = tuple(jax.ShapeDtypeStruct((m, width), BF16) for _, width, _ in _SEGS)
    out_specs = tuple(pl.BlockSpec((tm, width), lambda i: (i, 0)) for _, width, _ in _SEGS)
    return pl.pallas_call(
        _in_proj_kernel,
        grid=(m // tm,),
        in_specs=[
            pl.BlockSpec((tm, D_MODEL), lambda i: (i, 0)),
            pl.BlockSpec((1, D_MODEL), lambda i: (0, 0)),
            pl.BlockSpec((D_MODEL, W_IN_COLS), lambda i: (0, 0)),
        ],
        out_specs=out_specs,
        out_shape=out_shape,
        compiler_params=_params(1),
        name="in_proj",
    )(x, g, w)


def _na_bias_table(rpb):
    cols = np.arange(GRID_W)
    col_start = np.clip(cols - KW // 2, 0, GRID_W - KW)
    c2 = cols[None, :]
    valid = (c2 >= col_start[:, None]) & (c2 < col_start[:, None] + KW)
    dc = np.clip(c2 - cols[:, None] + (KW - 1), 0, 2 * KW - 2)
    e = jnp.where(jnp.asarray(valid)[None, None], rpb[:, :, dc].astype(F32) * LOG2E,
                  -jnp.inf)
    tabs = []
    for d in range(KH):
        t = e[:, KH - 1 - d:2 * KH - 1 - d]
        tabs.append(t.transpose(0, 2, 1, 3).reshape(
            H_A // NA_HEADS_PER_GROUP, NA_HEADS_PER_GROUP * GRID_W, KH * GRID_W))
    return jnp.stack(tabs)


def _na_kernel(q_ref, kwin, vwin, bias_ref, g_ref, o_ref, obuf, *, rows):
    i = pl.program_id(1)
    grp = NA_HEADS_PER_GROUP * HEAD_DIM
    win_row0 = jnp.clip((i - 1) * NA_ROWS, 0, rows - 3 * NA_ROWS)
    g = g_ref[...]
    head_of_row = lax.broadcasted_iota(jnp.int32, (grp, grp), 0) // HEAD_DIM
    head_of_lane = lax.broadcasted_iota(jnp.int32, (grp, grp), 1) // HEAD_DIM
    diag = head_of_row == head_of_lane
    lane_head = lax.broadcasted_iota(jnp.int32, (GRID_W, grp), 1) // HEAD_DIM

    def row_body(rr, carry):
        r = i * NA_ROWS + rr
        rs = jnp.clip(r - KH // 2, 0, rows - KH)
        d = r - rs
        off = pl.multiple_of((rs - win_row0) * GRID_W, GRID_W)
        qoff = pl.multiple_of(rr * GRID_W, GRID_W)
        for gi in range(H_A // NA_HEADS_PER_GROUP):
            lanes = slice(gi * grp, (gi + 1) * grp)
            q4 = q_ref[pl.ds(qoff, GRID_W), lanes]
            qbd = jnp.where(diag, jnp.concatenate([q4] * NA_HEADS_PER_GROUP, axis=0),
                            jnp.zeros((), BF16))
            k = kwin[pl.ds(off, KH * GRID_W), lanes]
            v = vwin[pl.ds(off, KH * GRID_W), lanes]
            s = lax.dot_general(qbd, k, (((1,), (1,)), ((), ())), preferred_element_type=F32)
            s = s + bias_ref[d, gi]
            m = jnp.max(s, axis=-1, keepdims=True)
            p = jnp.exp2(s - m)
            l = jnp.sum(p, axis=-1, keepdims=True)
            c = jnp.dot(p.astype(BF16), v, preferred_element_type=F32)
            c = c * (1.0 / l)
            out = c[0:GRID_W]
            for h in range(1, NA_HEADS_PER_GROUP):
                out = jnp.where(lane_head == h, c[h * GRID_W:(h + 1) * GRID_W], out)
            obuf[:, lanes] = out
        o_ref[pl.ds(qoff, GRID_W), :] = _rms(obuf[...], g).astype(BF16)
        return carry

    lax.fori_loop(0, NA_ROWS, row_body, 0, unroll=4)


def _na_attention(q, k, v, bias, g, batch, seq):
    rows = seq // GRID_W
    nblk = rows // NA_ROWS
    blk = NA_ROWS * GRID_W
    cur = lambda b, i: (b * nblk + i, 0)
    win = lambda b, i: (pl.multiple_of((b * nblk + jnp.clip(i - 1, 0, nblk - 3)) * blk, blk), 0)
    n_grp = H_A // NA_HEADS_PER_GROUP
    return pl.pallas_call(
        functools.partial(_na_kernel, rows=rows),
        grid=(batch, nblk),
        in_specs=[
            pl.BlockSpec((blk, D_A), cur),
            pl.BlockSpec((pl.Element(3 * blk), pl.Element(D_A)), win),
            pl.BlockSpec((pl.Element(3 * blk), pl.Element(D_A)), win),
            pl.BlockSpec((KH, n_grp, NA_HEADS_PER_GROUP * GRID_W, KH * GRID_W),
                         lambda b, i: (0, 0, 0, 0), pipeline_mode=pl.Buffered(1)),
            pl.BlockSpec((1, D_A), lambda b, i: (0, 0)),
        ],
        out_specs=pl.BlockSpec((blk, D_A), cur),
        out_shape=jax.ShapeDtypeStruct((batch * seq, D_A), BF16),
        scratch_shapes=[pltpu.VMEM((GRID_W, D_A), F32)],
        compiler_params=_params(2),
        name="na_attention",
    )(q, k, v, bias, g)


def _swa_penalty_table():
    tq = lax.broadcasted_iota(jnp.int32, (3, 1, BLOCK, 3 * BLOCK), 2)
    ts = lax.broadcasted_iota(jnp.int32, (3, 1, BLOCK, 3 * BLOCK), 3)
    shift = lax.broadcasted_iota(jnp.int32, (3, 1, BLOCK, 3 * BLOCK), 0) * BLOCK
    dist = jnp.abs(tq + shift - ts)
    slopes = jnp.exp2(-8.0 * jnp.arange(1, H_B + 1, dtype=F32) / H_B).reshape(1, H_B, 1, 1)
    return jnp.where(dist <= WINDOW, -(slopes * LOG2E) * dist.astype(F32), -jnp.inf)


def _swa_kernel(sink_ref, q_ref, kwin_ref, vwin_ref, pen_ref, g_ref, o_ref, k_var, v_var, *, nb):
    i = pl.program_id(1)
    nwin = SWA_BLOCKS + 2
    win0 = jnp.clip(SWA_BLOCKS * i - 1, 0, nb - nwin)

    kw = kwin_ref[...]
    vw = vwin_ref[...]
    kw_sw = pltpu.roll(kw, HEAD_DIM, 1)
    low = lax.broadcasted_iota(jnp.int32, (nwin * BLOCK, D_KV), 1) < HEAD_DIM
    zero = jnp.zeros((), BF16)
    k_var[0] = jnp.where(low, kw, zero)
    k_var[1] = jnp.where(low, zero, kw_sw)
    k_var[2] = jnp.where(low, kw_sw, zero)
    k_var[3] = jnp.where(low, zero, kw)
    ones = jnp.ones((nwin * BLOCK, D_KV), BF16)
    v_var[0] = jnp.concatenate([vw, ones], axis=1)
    v_var[1] = jnp.concatenate([pltpu.roll(vw, HEAD_DIM, 1), ones], axis=1)

    low_q = lax.broadcasted_iota(jnp.int32, (BLOCK, D_KV), 1) < HEAD_DIM
    g = g_ref[...]
    for u in range(SWA_BLOCKS):
        qb = SWA_BLOCKS * i + u
        wb = jnp.clip(qb - 1, 0, nb - 3)
        off = pl.multiple_of((wb - win0) * BLOCK, BLOCK)
        place = qb - wb
        rows = slice(u * BLOCK, (u + 1) * BLOCK)
        res = {}
        for kh in range(H_KV):
            for half in range(2):
                heads = (kh * GROUP + half, kh * GROUP + half + 2)
                qs = jnp.concatenate(
                    [q_ref[rows, (h // 2) * D_KV:(h // 2 + 1) * D_KV] for h in heads], axis=0)
                k = k_var[2 * kh + half, pl.ds(off, 3 * BLOCK), :]
                v = v_var[kh ^ half, pl.ds(off, 3 * BLOCK), :]
                s = lax.dot_general(qs, k, (((1,), (1,)), ((), ())),
                                    preferred_element_type=F32)
                es, sink_terms = [], []
                for n, h in enumerate(heads):
                    sink = sink_ref[h]
                    logits = s[n * BLOCK:(n + 1) * BLOCK] + pen_ref[place, h]
                    m = jnp.maximum(jnp.max(logits, axis=-1, keepdims=True), sink)
                    es.append(jnp.exp2(logits - m).astype(BF16))
                    sink_terms.append(jnp.exp2(sink - m))
                c = jnp.dot(jnp.concatenate(es, axis=0), v, preferred_element_type=F32)
                for n, h in enumerate(heads):
                    cn = c[n * BLOCK:(n + 1) * BLOCK]
                    res[h] = cn[:, :D_KV] * (1.0 / (cn[:, D_KV:] + sink_terms[n]))
        out = jnp.concatenate(
            [jnp.where(low_q, res[2 * j], res[2 * j + 1]) for j in range(H_B // 2)], axis=1)
        o_ref[rows, :] = _rms(out, g).astype(BF16)


def _swa_attention(q, k, v, sinks, pen, g, batch, seq):
    nb = seq // BLOCK
    nsteps = nb // SWA_BLOCKS
    nwin = SWA_BLOCKS + 2
    cur = lambda b, i: (b * nsteps + i, 0)
    win = lambda b, i: (pl.multiple_of(
        (b * nb + jnp.clip(SWA_BLOCKS * i - 1, 0, nb - nwin)) * BLOCK, BLOCK), 0)
    return pl.pallas_call(
        functools.partial(_swa_kernel, nb=nb),
        grid=(batch, nsteps),
        in_specs=[
            pl.BlockSpec(memory_space=pltpu.SMEM),
            pl.BlockSpec((SWA_BLOCKS * BLOCK, D_B), cur),
            pl.BlockSpec((pl.Element(nwin * BLOCK), pl.Element(D_KV)), win),
            pl.BlockSpec((pl.Element(nwin * BLOCK), pl.Element(D_KV)), win),
            pl.BlockSpec((3, H_B, BLOCK, 3 * BLOCK), lambda b, i: (0, 0, 0, 0),
                         pipeline_mode=pl.Buffered(1)),
            pl.BlockSpec((1, D_B), lambda b, i: (0, 0)),
        ],
        out_specs=pl.BlockSpec((SWA_BLOCKS * BLOCK, D_B), cur),
        out_shape=jax.ShapeDtypeStruct((batch * seq, D_B), BF16),
        scratch_shapes=[
            pltpu.VMEM((4, nwin * BLOCK, D_KV), BF16),
            pltpu.VMEM((2, nwin * BLOCK, 2 * D_KV), BF16),
        ],
        compiler_params=_params(2),
        name="swa_attention",
    )(sinks, q, k, v, pen, g)


def _out_proj_kernel(x_ref, a_ref, b_ref, w_ref, o_ref):
    acc = jnp.dot(a_ref[...], w_ref[0:D_A, :], preferred_element_type=F32)
    acc = acc + jnp.dot(b_ref[...], w_ref[D_A:, :], preferred_element_type=F32)
    o_ref[...] = x_ref[...] + acc


def _out_proj(x, a, b, w):
    m = x.shape[0]
    tm = TM_PROJ
    return pl.pallas_call(
        _out_proj_kernel,
        grid=(m // tm,),
        in_specs=[
            pl.BlockSpec((tm, D_MODEL), lambda i: (i, 0)),
            pl.BlockSpec((tm, D_A), lambda i: (i, 0)),
            pl.BlockSpec((tm, D_B), lambda i: (i, 0)),
            pl.BlockSpec((D_A + D_B, D_MODEL), lambda i: (0, 0)),
        ],
        out_specs=pl.BlockSpec((tm, D_MODEL), lambda i: (i, 0)),
        out_shape=jax.ShapeDtypeStruct((m, D_MODEL), F32),
        compiler_params=_params(1),
        name="out_proj",
    )(x, a, b, w)


def _ffn_kernel(xp_ref, x_ref, xn_ref, g_ref, wup_ref, cw_ref, cb_ref, wd_ref, gf_ref,
                o_ref, h_ext, act, *, tiles_per_seq, final):
    i = pl.program_id(0)
    tm = x_ref.shape[0]
    ext = tm + 2 * HALO
    g = g_ref[...]
    t = i % tiles_per_seq
    keep_prev = (t > 0).astype(F32)
    keep_next = (t < tiles_per_seq - 1).astype(F32)
    h_ext[0:HALO] = (_rms(xp_ref[...], g) * keep_prev).astype(BF16)
    h_ext[HALO:HALO + tm] = _rms(x_ref[...], g).astype(BF16)
    h_ext[HALO + tm:ext] = (_rms(xn_ref[...], g) * keep_next).astype(BF16)

    def conv(u, col0):
        cols = slice(col0, col0 + FF_CHUNK)
        up = pltpu.roll(u, 1, 0)[HALO:HALO + tm]
        un = pltpu.roll(u, ext - 1, 0)[HALO:HALO + tm]
        uc = u[HALO:HALO + tm]
        return (up * cw_ref[0:1, cols] + uc * cw_ref[1:2, cols] + un * cw_ref[2:3, cols]
                + cb_ref[0:1, cols])

    for c in range(D_FF // FF_CHUNK):
        gc0 = c * FF_CHUNK
        vc0 = D_FF + c * FF_CHUNK
        hx = h_ext[...]
        ug = jnp.dot(hx, wup_ref[:, gc0:gc0 + FF_CHUNK], preferred_element_type=F32)
        uv = jnp.dot(hx, wup_ref[:, vc0:vc0 + FF_CHUNK], preferred_element_type=F32)
        gate = conv(ug, gc0)
        val = conv(uv, vc0)
        act[:, gc0:gc0 + FF_CHUNK] = (gate * (1.0 / (1.0 + jnp.exp(-gate))) * val).astype(BF16)

    y = x_ref[...] + jnp.dot(act[...], wd_ref[...], preferred_element_type=F32)
    if final:
        y = _rms(y, gf_ref[...])
    o_ref[...] = y


def _ffn(x, g, w_up, conv_w, conv_b, w_down, g_final, seq, final):
    m = x.shape[0]
    tm = TM_FFN
    tiles_per_seq = seq // tm
    hb = tm // HALO
    n_halo_blocks = m // HALO
    const = lambda i: (0, 0)
    resident = dict(pipeline_mode=pl.Buffered(1))
    return pl.pallas_call(
        functools.partial(_ffn_kernel, tiles_per_seq=tiles_per_seq, final=final),
        grid=(m // tm,),
        in_specs=[
            pl.BlockSpec((HALO, D_MODEL), lambda i: (jnp.maximum(i * hb - 1, 0), 0)),
            pl.BlockSpec((tm, D_MODEL), lambda i: (i, 0)),
            pl.BlockSpec((HALO, D_MODEL), lambda i: (jnp.minimum((i + 1) * hb, n_halo_blocks - 1), 0)),
            pl.BlockSpec((1, D_MODEL), const),
            pl.BlockSpec((D_MODEL, 2 * D_FF), const, **resident),
            pl.BlockSpec((3, 2 * D_FF), const),
            pl.BlockSpec((1, 2 * D_FF), const),
            pl.BlockSpec((D_FF, D_MODEL), const, **resident),
            pl.BlockSpec((1, D_MODEL), const),
        ],
        out_specs=pl.BlockSpec((tm, D_MODEL), lambda i: (i, 0)),
        out_shape=jax.ShapeDtypeStruct((m, D_MODEL), F32),
        scratch_shapes=[
            pltpu.VMEM((tm + 2 * HALO, D_MODEL), BF16),
            pltpu.VMEM((tm, D_FF), BF16),
        ],
        compiler_params=_params(1),
        name="ffn",
    )(x, x, x, g, w_up, conv_w, conv_b, w_down, g_final)


def _trunk(x, p):
    batch, seq, _ = x.shape
    depth = p["w_in"].shape[0]
    xf = x.reshape(batch * seq, D_MODEL)
    for l in range(depth):
        qa, ka, va, qb, kb, vb = _in_proj(xf, p["norm_mix"][l], p["w_in"][l])
        oa = _na_attention(qa, ka, va, p["na_bias"][l], p["norm_grp_a"][l], batch, seq)
        ob = _swa_attention(qb, kb, vb, p["sinks"][l], p["swa_pen"], p["norm_grp_b"][l],
                            batch, seq)
        xf = _out_proj(xf, oa, ob, p["w_out"][l])
        xf = _ffn(xf, p["norm_ffn"][l], p["w_up"][l], p["conv_w"][l], p["conv_b"][l],
                  p["w_down"][l], p["norm_final"], seq, final=(l == depth - 1))
    return xf.reshape(batch, seq, D_MODEL)


def kernel(x_prompt, x_sample, norm_mix, w_in, rpb, sinks, norm_grp, w_out, norm_ffn, w_up,
           conv_w, conv_b, w_down, norm_final):
    depth = w_in.shape[0]
    p = {
        "norm_mix": norm_mix.reshape(depth, 1, D_MODEL),
        "w_in": w_in.astype(BF16),
        "na_bias": jnp.stack([_na_bias_table(rpb[l]) for l in range(depth)]),
        "sinks": sinks.astype(F32) * LOG2E,
        "swa_pen": _swa_penalty_table(),
        "norm_grp_a": norm_grp[:, :D_A].reshape(depth, 1, D_A),
        "norm_grp_b": norm_grp[:, D_A:].reshape(depth, 1, D_B),
        "w_out": w_out.astype(BF16),
        "norm_ffn": norm_ffn.reshape(depth, 1, D_MODEL),
        "w_up": w_up.astype(BF16),
        "conv_w": conv_w,
        "conv_b": conv_b.reshape(depth, 1, 2 * D_FF),
        "w_down": w_down.astype(BF16),
        "norm_final": norm_final.reshape(1, D_MODEL),
    }
    return (_trunk(x_prompt, p), _trunk(x_sample, p))
```

```python
import functools
import math

import jax
import jax.numpy as jnp
import numpy as np
from jax import lax
from jax.experimental import pallas as pl
from jax.experimental.pallas import tpu as pltpu

D_MODEL = 1024
HEAD_DIM = 64
H_A = 8
D_A = H_A * HEAD_DIM
H_B = 8
H_KV = 2
GROUP = H_B // H_KV
D_B = H_B * HEAD_DIM
D_KV = H_KV * HEAD_DIM
D_MIX = D_A + D_B
W_IN_COLS = 3 * D_A + D_B + 2 * D_KV
GRID_W = 64
KH = 8
KW = 16
WINDOW = 128
BLOCK = 128
D_FF = 2816
EPS = 1e-6
SCALE = 1.0 / math.sqrt(HEAD_DIM)
LOG2E = math.log2(math.e)
Q_SCALE = SCALE * LOG2E

F32 = jnp.float32
BF16 = jnp.bfloat16

VMEM_LIMIT_BYTES = 56 * 1024 * 1024

TM_PROJ = 1024
TM_FFN = 1024
HALO = 16
FF_CHUNK = 256
NA_ROWS = 8
NA_HEADS_PER_GROUP = 4
SWA_BLOCKS = 4


def _rms(x, g):
    inv = lax.rsqrt(jnp.mean(x * x, axis=-1, keepdims=True) + EPS)
    return (x * inv) * g


def _params(n_axes):
    return pltpu.CompilerParams(
        dimension_semantics=("parallel",) * n_axes,
        vmem_limit_bytes=VMEM_LIMIT_BYTES,
    )


_SEGS = (
    (0, D_A, Q_SCALE),
    (D_A, D_A, None),
    (2 * D_A, D_A, None),
    (3 * D_A, D_B, Q_SCALE),
    (3 * D_A + D_B, D_KV, None),
    (3 * D_A + D_B + D_KV, D_KV, None),
)


def _in_proj_kernel(x_ref, g_ref, w_ref, *out_refs):
    h = _rms(x_ref[...], g_ref[...]).astype(BF16)
    for (start, width, scale), o_ref in zip(_SEGS, out_refs):
        p = jnp.dot(h, w_ref[:, start:start + width], preferred_element_type=F32)
        if scale is not None:
            p = p * scale
        o_ref[...] = p.astype(BF16)


def _in_proj(x, g, w, layer):
    m = x.shape[0]
    tm = TM_PROJ
    lay = lambda i: (layer, 0, 0)
    out_shape = tuple(jax.ShapeDtypeStruct((m, width), BF16) for _, width, _ in _SEGS)
    out_specs = tuple(pl.BlockSpec((tm, width), lambda i: (i, 0)) for _, width, _ in _SEGS)
    return pl.pallas_call(
        _in_proj_kernel,
        grid=(m // tm,),
        in_specs=[
            pl.BlockSpec((tm, D_MODEL), lambda i: (i, 0)),
            pl.BlockSpec((None, 1, D_MODEL), lay),
            pl.BlockSpec((None, D_MODEL, W_IN_COLS), lay),
        ],
        out_specs=out_specs,
        out_shape=out_shape,
        compiler_params=_params(1),
        name="in_proj",
    )(x, g, w)


def _na_bias_table(rpb):
    cols = np.arange(GRID_W)
    col_start = np.clip(cols - KW // 2, 0, GRID_W - KW)
    c2 = cols[None, :]
    valid = (c2 >= col_start[:, None]) & (c2 < col_start[:, None] + KW)
    dc = np.clip(c2 - cols[:, None] + (KW - 1), 0, 2 * KW - 2)
    e = jnp.where(jnp.asarray(valid)[None, None], rpb[:, :, dc].astype(F32) * LOG2E,
                  -jnp.inf)
    tabs = []
    for d in range(KH):
        t = e[:, KH - 1 - d:2 * KH - 1 - d]
        tabs.append(t.transpose(0, 2, 1, 3).reshape(
            H_A // NA_HEADS_PER_GROUP, NA_HEADS_PER_GROUP * GRID_W, KH * GRID_W))
    return jnp.stack(tabs)


def _na_kernel(q_ref, kwin, vwin, bias_ref, g_ref, o_ref, *, rows):
    i = pl.program_id(1)
    grp = NA_HEADS_PER_GROUP * HEAD_DIM
    win_row0 = jnp.clip((i - 1) * NA_ROWS, 0, rows - 3 * NA_ROWS)
    g = g_ref[...]
    head_of_row = lax.broadcasted_iota(jnp.int32, (grp, grp), 0) // HEAD_DIM
    head_of_lane = lax.broadcasted_iota(jnp.int32, (grp, grp), 1) // HEAD_DIM
    diag = head_of_row == head_of_lane
    lane_head = lax.broadcasted_iota(jnp.int32, (GRID_W, grp), 1) // HEAD_DIM

    def row_body(rr, carry):
        r = i * NA_ROWS + rr
        rs = jnp.clip(r - KH // 2, 0, rows - KH)
        d = r - rs
        off = pl.multiple_of((rs - win_row0) * GRID_W, GRID_W)
        qoff = pl.multiple_of(rr * GRID_W, GRID_W)
        outs = []
        for gi in range(H_A // NA_HEADS_PER_GROUP):
            lanes = slice(gi * grp, (gi + 1) * grp)
            q4 = q_ref[pl.ds(qoff, GRID_W), lanes]
            qbd = jnp.where(diag, jnp.concatenate([q4] * NA_HEADS_PER_GROUP, axis=0),
                            jnp.zeros((), BF16))
            k = kwin[pl.ds(off, KH * GRID_W), lanes]
            v = vwin[pl.ds(off, KH * GRID_W), lanes]
            s = lax.dot_general(qbd, k, (((1,), (1,)), ((), ())), preferred_element_type=F32)
            s = s + bias_ref[d, gi]
            m = jnp.max(s, axis=-1, keepdims=True)
            p = jnp.exp2(s - m)
            l = jnp.sum(p, axis=-1, keepdims=True)
            c = jnp.dot(p.astype(BF16), v, preferred_element_type=F32)
            c = c * (1.0 / l)
            out = c[0:GRID_W]
            for h in range(1, NA_HEADS_PER_GROUP):
                out = jnp.where(lane_head == h, c[h * GRID_W:(h + 1) * GRID_W], out)
            outs.append(out)
        o_ref[pl.ds(qoff, GRID_W), :] = _rms(jnp.concatenate(outs, axis=1), g).astype(BF16)
        return carry

    lax.fori_loop(0, NA_ROWS, row_body, 0, unroll=8)


def _na_attention(q, k, v, bias, g, layer, batch, seq):
    rows = seq // GRID_W
    nblk = rows // NA_ROWS
    blk = NA_ROWS * GRID_W
    cur = lambda b, i: (b * nblk + i, 0)
    win = lambda b, i: (pl.multiple_of((b * nblk + jnp.clip(i - 1, 0, nblk - 3)) * blk, blk), 0)
    n_grp = H_A // NA_HEADS_PER_GROUP
    return pl.pallas_call(
        functools.partial(_na_kernel, rows=rows),
        grid=(batch, nblk),
        in_specs=[
            pl.BlockSpec((blk, D_A), cur),
            pl.BlockSpec((pl.Element(3 * blk), pl.Element(D_A)), win),
            pl.BlockSpec((pl.Element(3 * blk), pl.Element(D_A)), win),
            pl.BlockSpec((None, KH, n_grp, NA_HEADS_PER_GROUP * GRID_W, KH * GRID_W),
                         lambda b, i: (layer, 0, 0, 0, 0), pipeline_mode=pl.Buffered(1)),
            pl.BlockSpec((None, 1, D_A), lambda b, i: (layer, 0, 0)),
        ],
        out_specs=pl.BlockSpec((blk, D_A), cur),
        out_shape=jax.ShapeDtypeStruct((batch * seq, D_A), BF16),
        compiler_params=_params(2),
        name="na_attention",
    )(q, k, v, bias, g)


def _swa_penalty_table():
    tq = lax.broadcasted_iota(jnp.int32, (3, 1, BLOCK, 3 * BLOCK), 2)
    ts = lax.broadcasted_iota(jnp.int32, (3, 1, BLOCK, 3 * BLOCK), 3)
    shift = lax.broadcasted_iota(jnp.int32, (3, 1, BLOCK, 3 * BLOCK), 0) * BLOCK
    dist = jnp.abs(tq + shift - ts)
    slopes = jnp.exp2(-8.0 * jnp.arange(1, H_B + 1, dtype=F32) / H_B).reshape(1, H_B, 1, 1)
    return jnp.where(dist <= WINDOW, -(slopes * LOG2E) * dist.astype(F32), -jnp.inf)


def _swa_kernel(sink_ref, q_ref, kwin_ref, vwin_ref, pen_ref, g_ref, o_ref, k_var, v_var, *, nb):
    i = pl.program_id(1)
    nwin = SWA_BLOCKS + 2
    win0 = jnp.clip(SWA_BLOCKS * i - 1, 0, nb - nwin)

    kw = kwin_ref[...]
    vw = vwin_ref[...]
    kw_sw = pltpu.roll(kw, HEAD_DIM, 1)
    low = lax.broadcasted_iota(jnp.int32, (nwin * BLOCK, D_KV), 1) < HEAD_DIM
    zero = jnp.zeros((), BF16)
    k_var[0] = jnp.where(low, kw, zero)
    k_var[1] = jnp.where(low, zero, kw_sw)
    k_var[2] = jnp.where(low, kw_sw, zero)
    k_var[3] = jnp.where(low, zero, kw)
    ones = jnp.ones((nwin * BLOCK, D_KV), BF16)
    v_var[0] = jnp.concatenate([vw, ones], axis=1)
    v_var[1] = jnp.concatenate([pltpu.roll(vw, HEAD_DIM, 1), ones], axis=1)

    low_q = lax.broadcasted_iota(jnp.int32, (BLOCK, D_KV), 1) < HEAD_DIM
    g = g_ref[...]
    for u in range(SWA_BLOCKS):
        qb = SWA_BLOCKS * i + u
        wb = jnp.clip(qb - 1, 0, nb - 3)
        off = pl.multiple_of((wb - win0) * BLOCK, BLOCK)
        place = qb - wb
        rows = slice(u * BLOCK, (u + 1) * BLOCK)
        res = {}
        for kh in range(H_KV):
            for half in range(2):
                heads = (kh * GROUP + half, kh * GROUP + half + 2)
                qs = jnp.concatenate(
                    [q_ref[rows, (h // 2) * D_KV:(h // 2 + 1) * D_KV] for h in heads], axis=0)
                k = k_var[2 * kh + half, pl.ds(off, 3 * BLOCK), :]
                v = v_var[kh ^ half, pl.ds(off, 3 * BLOCK), :]
                s = lax.dot_general(qs, k, (((1,), (1,)), ((), ())),
                                    preferred_element_type=F32)
                es, sink_terms = [], []
                for n, h in enumerate(heads):
                    sink = sink_ref[h]
                    logits = s[n * BLOCK:(n + 1) * BLOCK] + pen_ref[place, h]
                    m = jnp.maximum(jnp.max(logits, axis=-1, keepdims=True), sink)
                    es.append(jnp.exp2(logits - m).astype(BF16))
                    sink_terms.append(jnp.exp2(sink - m))
                c = jnp.dot(jnp.concatenate(es, axis=0), v, preferred_element_type=F32)
                for n, h in enumerate(heads):
                    cn = c[n * BLOCK:(n + 1) * BLOCK]
                    res[h] = cn[:, :D_KV] * (1.0 / (cn[:, D_KV:] + sink_terms[n]))
        out = jnp.concatenate(
            [jnp.where(low_q, res[2 * j], res[2 * j + 1]) for j in range(H_B // 2)], axis=1)
        o_ref[rows, :] = _rms(out, g).astype(BF16)


def _swa_attention(q, k, v, sinks, pen, g, layer, batch, seq):
    nb = seq // BLOCK
    nsteps = nb // SWA_BLOCKS
    nwin = SWA_BLOCKS + 2
    cur = lambda b, i: (b * nsteps + i, 0)
    win = lambda b, i: (pl.multiple_of(
        (b * nb + jnp.clip(SWA_BLOCKS * i - 1, 0, nb - nwin)) * BLOCK, BLOCK), 0)
    return pl.pallas_call(
        functools.partial(_swa_kernel, nb=nb),
        grid=(batch, nsteps),
        in_specs=[
            pl.BlockSpec(memory_space=pltpu.SMEM),
            pl.BlockSpec((SWA_BLOCKS * BLOCK, D_B), cur),
            pl.BlockSpec((pl.Element(nwin * BLOCK), pl.Element(D_KV)), win),
            pl.BlockSpec((pl.Element(nwin * BLOCK), pl.Element(D_KV)), win),
            pl.BlockSpec((3, H_B, BLOCK, 3 * BLOCK), lambda b, i: (0, 0, 0, 0),
                         pipeline_mode=pl.Buffered(1)),
            pl.BlockSpec((None, 1, D_B), lambda b, i: (layer, 0, 0)),
        ],
        out_specs=pl.BlockSpec((SWA_BLOCKS * BLOCK, D_B), cur),
        out_shape=jax.ShapeDtypeStruct((batch * seq, D_B), BF16),
        scratch_shapes=[
            pltpu.VMEM((4, nwin * BLOCK, D_KV), BF16),
            pltpu.VMEM((2, nwin * BLOCK, 2 * D_KV), BF16),
        ],
        compiler_params=_params(2),
        name="swa_attention",
    )(sinks, q, k, v, pen, g)


def _mix_ffn_kernel(xp_ref, x_ref, xn_ref, ap_ref, a_ref, an_ref, bp_ref, b_ref, bn_ref,
                    wo_ref, g_ref, wup_ref, cw_ref, cb_ref, wd_ref, gf_ref,
                    o_ref, ab_ext, xmid, h_ext, act, *, tiles_per_seq, final):
    i = pl.program_id(0)
    tm = x_ref.shape[0]
    ext = tm + 2 * HALO
    main = slice(HALO, HALO + tm)
    g = g_ref[...]
    t = i % tiles_per_seq
    keep_prev = (t > 0).astype(F32)
    keep_next = (t < tiles_per_seq - 1).astype(F32)

    ab_ext[0:HALO, 0:D_A] = ap_ref[...]
    ab_ext[0:HALO, D_A:] = bp_ref[...]
    ab_ext[main, 0:D_A] = a_ref[...]
    ab_ext[main, D_A:] = b_ref[...]
    ab_ext[HALO + tm:ext, 0:D_A] = an_ref[...]
    ab_ext[HALO + tm:ext, D_A:] = bn_ref[...]
    mix = jnp.dot(ab_ext[...], wo_ref[...], preferred_element_type=F32)
    x_mid = x_ref[...] + mix[main]
    xmid[...] = x_mid
    h_ext[0:HALO] = (_rms(xp_ref[...] + mix[0:HALO], g) * keep_prev).astype(BF16)
    h_ext[main] = _rms(x_mid, g).astype(BF16)
    h_ext[HALO + tm:ext] = (_rms(xn_ref[...] + mix[HALO + tm:ext], g) * keep_next).astype(BF16)

    def conv(u, col0):
        cols = slice(col0, col0 + FF_CHUNK)
        up = pltpu.roll(u, 1, 0)[main]
        un = pltpu.roll(u, ext - 1, 0)[main]
        return (up * cw_ref[0:1, cols] + u[main] * cw_ref[1:2, cols] + un * cw_ref[2:3, cols]
                + cb_ref[0:1, cols])

    for c in range(D_FF // FF_CHUNK):
        gc0 = c * FF_CHUNK
        vc0 = D_FF + c * FF_CHUNK
        hx = h_ext[...]
        ug = jnp.dot(hx, wup_ref[:, gc0:gc0 + FF_CHUNK], preferred_element_type=F32)
        uv = jnp.dot(hx, wup_ref[:, vc0:vc0 + FF_CHUNK], preferred_element_type=F32)
        gate = conv(ug, gc0)
        val = conv(uv, vc0)
        act[:, gc0:gc0 + FF_CHUNK] = (gate * (1.0 / (1.0 + jnp.exp(-gate))) * val).astype(BF16)

    y = xmid[...] + jnp.dot(act[...], wd_ref[...], preferred_element_type=F32)
    if final:
        y = _rms(y, gf_ref[...])
    o_ref[...] = y


def _mix_ffn(x, a, b, p, layer, seq, final):
    m = x.shape[0]
    tm = TM_FFN
    tiles_per_seq = seq // tm
    hb = tm // HALO
    n_halo_blocks = m // HALO
    prev = lambda i: (jnp.maximum(i * hb - 1, 0), 0)
    cur = lambda i: (i, 0)
    nxt = lambda i: (jnp.minimum((i + 1) * hb, n_halo_blocks - 1), 0)
    lay = lambda i: (layer, 0, 0)
    resident = dict(pipeline_mode=pl.Buffered(1))

    def halo3(width):
        return [pl.BlockSpec((HALO, width), prev), pl.BlockSpec((tm, width), cur),
                pl.BlockSpec((HALO, width), nxt)]

    return pl.pallas_call(
        functools.partial(_mix_ffn_kernel, tiles_per_seq=tiles_per_seq, final=final),
        grid=(m // tm,),
        in_specs=halo3(D_MODEL) + halo3(D_A) + halo3(D_B) + [
            pl.BlockSpec((None, D_MIX, D_MODEL), lay, **resident),
            pl.BlockSpec((None, 1, D_MODEL), lay),
            pl.BlockSpec((None, D_MODEL, 2 * D_FF), lay, **resident),
            pl.BlockSpec((None, 3, 2 * D_FF), lay),
            pl.BlockSpec((None, 1, 2 * D_FF), lay),
            pl.BlockSpec((None, D_FF, D_MODEL), lay, **resident),
            pl.BlockSpec((1, D_MODEL), lambda i: (0, 0)),
        ],
        out_specs=pl.BlockSpec((tm, D_MODEL), cur),
        out_shape=jax.ShapeDtypeStruct((m, D_MODEL), F32),
        scratch_shapes=[
            pltpu.VMEM((tm + 2 * HALO, D_MIX), BF16),
            pltpu.VMEM((tm, D_MODEL), F32),
            pltpu.VMEM((tm + 2 * HALO, D_MODEL), BF16),
            pltpu.VMEM((tm, D_FF), BF16),
        ],
        compiler_params=_params(1),
        name="mix_ffn",
    )(x, x, x, a, a, a, b, b, b, p["w_out"], p["norm_ffn"], p["w_up"], p["conv_w"],
      p["conv_b"], p["w_down"], p["norm_final"])


def _trunk(x, p):
    batch, seq, _ = x.shape
    depth = p["w_in"].shape[0]
    xf = x.reshape(batch * seq, D_MODEL)
    for l in range(depth):
        qa, ka, va, qb, kb, vb = _in_proj(xf, p["norm_mix"], p["w_in"], l)
        oa = _na_attention(qa, ka, va, p["na_bias"], p["norm_grp_a"], l, batch, seq)
        ob = _swa_attention(qb, kb, vb, p["sinks"][l], p["swa_pen"], p["norm_grp_b"], l,
                            batch, seq)
        xf = _mix_ffn(xf, oa, ob, p, l, seq, final=(l == depth - 1))
    return xf.reshape(batch, seq, D_MODEL)


def kernel(x_prompt, x_sample, norm_mix, w_in, rpb, sinks, norm_grp, w_out, norm_ffn, w_up,
           conv_w, conv_b, w_down, norm_final):
    depth = w_in.shape[0]
    p = {
        "norm_mix": norm_mix.reshape(depth, 1, D_MODEL),
        "w_in": w_in.astype(BF16),
        "na_bias": jnp.stack([_na_bias_table(rpb[l]) for l in range(depth)]),
        "sinks": sinks.astype(F32) * LOG2E,
        "swa_pen": _swa_penalty_table(),
        "norm_grp_a": norm_grp[:, :D_A].reshape(depth, 1, D_A),
        "norm_grp_b": norm_grp[:, D_A:].reshape(depth, 1, D_B),
        "w_out": w_out.astype(BF16),
        "norm_ffn": norm_ffn.reshape(depth, 1, D_MODEL),
        "w_up": w_up.astype(BF16),
        "conv_w": conv_w,
        "conv_b": conv_b.reshape(depth, 1, 2 * D_FF),
        "w_down": w_down.astype(BF16),
        "norm_final": norm_final.reshape(1, D_MODEL),
    }
    return (_trunk(x_prompt, p), _trunk(x_sample, p))
```

```python
import functools
import math

import jax
import jax.numpy as jnp
import numpy as np
from jax import lax
from jax.experimental import pallas as pl
from jax.experimental.pallas import tpu as pltpu

D_MODEL = 1024
HEAD_DIM = 64
H_A = 8
D_A = H_A * HEAD_DIM
H_B = 8
H_KV = 2
GROUP = H_B // H_KV
D_B = H_B * HEAD_DIM
D_KV = H_KV * HEAD_DIM
D_MIX = D_A + D_B
W_IN_COLS = 3 * D_A + D_B + 2 * D_KV
GRID_W = 64
KH = 8
KW = 16
WINDOW = 128
BLOCK = 128
D_FF = 2816
EPS = 1e-6
SCALE = 1.0 / math.sqrt(HEAD_DIM)
LOG2E = math.log2(math.e)
Q_SCALE = SCALE * LOG2E

F32 = jnp.float32
BF16 = jnp.bfloat16

VMEM_LIMIT_BYTES = 56 * 1024 * 1024

TM_PROJ = 1024
TM_FFN = 1024
HALO = 16
FF_CHUNK = 256
NA_ROWS = 8
NA_HEADS_PER_GROUP = 4
SWA_BLOCKS = 8


def _rms(x, g):
    inv = lax.rsqrt(jnp.mean(x * x, axis=-1, keepdims=True) + EPS)
    return (x * inv) * g


def _params(n_axes):
    return pltpu.CompilerParams(
        dimension_semantics=("parallel",) * n_axes,
        vmem_limit_bytes=VMEM_LIMIT_BYTES,
    )


_SEGS = (
    (0, D_A, Q_SCALE),
    (D_A, D_A, None),
    (2 * D_A, D_A, None),
    (3 * D_A, D_B, Q_SCALE),
    (3 * D_A + D_B, D_KV, None),
    (3 * D_A + D_B + D_KV, D_KV, None),
)


def _in_proj_kernel(x_ref, g_ref, w_ref, *out_refs):
    h = _rms(x_ref[...], g_ref[...]).astype(BF16)
    for (start, width, scale), o_ref in zip(_SEGS, out_refs):
        p = jnp.dot(h, w_ref[:, start:start + width], preferred_element_type=F32)
        if scale is not None:
            p = p * scale
        o_ref[...] = p.astype(BF16)


def _in_proj(x, g, w, layer):
    m = x.shape[0]
    tm = TM_PROJ
    lay = lambda i: (layer, 0, 0)
    out_shape = tuple(jax.ShapeDtypeStruct((m, width), BF16) for _, width, _ in _SEGS)
    out_specs = tuple(pl.BlockSpec((tm, width), lambda i: (i, 0)) for _, width, _ in _SEGS)
    return pl.pallas_call(
        _in_proj_kernel,
        grid=(m // tm,),
        in_specs=[
            pl.BlockSpec((tm, D_MODEL), lambda i: (i, 0)),
            pl.BlockSpec((None, 1, D_MODEL), lay),
            pl.BlockSpec((None, D_MODEL, W_IN_COLS), lay),
        ],
        out_specs=out_specs,
        out_shape=out_shape,
        compiler_params=_params(1),
        name="in_proj",
    )(x, g, w)


def _na_bias_table(rpb):
    cols = np.arange(GRID_W)
    col_start = np.clip(cols - KW // 2, 0, GRID_W - KW)
    c2 = cols[None, :]
    valid = (c2 >= col_start[:, None]) & (c2 < col_start[:, None] + KW)
    dc = np.clip(c2 - cols[:, None] + (KW - 1), 0, 2 * KW - 2)
    onehot = jnp.asarray(dc[None] == np.arange(2 * KW - 1)[:, None, None], dtype=F32)
    picked = jnp.einsum("hek,kcd->hecd", rpb.astype(F32), onehot,
                        precision=lax.Precision.HIGHEST)
    e = jnp.where(jnp.asarray(valid)[None, None], picked * LOG2E, -jnp.inf)
    tabs = []
    for d in range(KH):
        t = e[:, KH - 1 - d:2 * KH - 1 - d]
        tabs.append(t.transpose(0, 2, 1, 3).reshape(
            H_A // NA_HEADS_PER_GROUP, NA_HEADS_PER_GROUP * GRID_W, KH * GRID_W))
    return jnp.stack(tabs)


def _na_kernel(q_ref, kwin, vwin, bias_ref, g_ref, o_ref, *, rows):
    i = pl.program_id(1)
    grp = NA_HEADS_PER_GROUP * HEAD_DIM
    win_row0 = jnp.clip((i - 1) * NA_ROWS, 0, rows - 3 * NA_ROWS)
    g = g_ref[...]
    head_of_row = lax.broadcasted_iota(jnp.int32, (grp, grp), 0) // HEAD_DIM
    head_of_lane = lax.broadcasted_iota(jnp.int32, (grp, grp), 1) // HEAD_DIM
    diag = head_of_row == head_of_lane
    lane_head = lax.broadcasted_iota(jnp.int32, (GRID_W, grp), 1) // HEAD_DIM

    def row_body(rr, carry):
        r = i * NA_ROWS + rr
        rs = jnp.clip(r - KH // 2, 0, rows - KH)
        d = r - rs
        off = pl.multiple_of((rs - win_row0) * GRID_W, GRID_W)
        qoff = pl.multiple_of(rr * GRID_W, GRID_W)
        outs = []
        for gi in range(H_A // NA_HEADS_PER_GROUP):
            lanes = slice(gi * grp, (gi + 1) * grp)
            q4 = q_ref[pl.ds(qoff, GRID_W), lanes]
            qbd = jnp.where(diag, jnp.concatenate([q4] * NA_HEADS_PER_GROUP, axis=0),
                            jnp.zeros((), BF16))
            k = kwin[pl.ds(off, KH * GRID_W), lanes]
            v = vwin[pl.ds(off, KH * GRID_W), lanes]
            s = lax.dot_general(qbd, k, (((1,), (1,)), ((), ())), preferred_element_type=F32)
            s = s + bias_ref[d, gi]
            m = jnp.max(s, axis=-1, keepdims=True)
            p = jnp.exp2(s - m)
            l = jnp.sum(p, axis=-1, keepdims=True)
            c = jnp.dot(p.astype(BF16), v, preferred_element_type=F32)
            c = c * (1.0 / l)
            out = c[0:GRID_W]
            for h in range(1, NA_HEADS_PER_GROUP):
                out = jnp.where(lane_head == h, c[h * GRID_W:(h + 1) * GRID_W], out)
            outs.append(out)
        o_ref[pl.ds(qoff, GRID_W), :] = _rms(jnp.concatenate(outs, axis=1), g).astype(BF16)
        return carry

    lax.fori_loop(0, NA_ROWS, row_body, 0, unroll=8)


def _na_attention(q, k, v, bias, g, layer, batch, seq):
    rows = seq // GRID_W
    nblk = rows // NA_ROWS
    blk = NA_ROWS * GRID_W
    cur = lambda b, i: (b * nblk + i, 0)
    win = lambda b, i: (pl.multiple_of((b * nblk + jnp.clip(i - 1, 0, nblk - 3)) * blk, blk), 0)
    n_grp = H_A // NA_HEADS_PER_GROUP
    return pl.pallas_call(
        functools.partial(_na_kernel, rows=rows),
        grid=(batch, nblk),
        in_specs=[
            pl.BlockSpec((blk, D_A), cur),
            pl.BlockSpec((pl.Element(3 * blk), pl.Element(D_A)), win),
            pl.BlockSpec((pl.Element(3 * blk), pl.Element(D_A)), win),
            pl.BlockSpec((None, KH, n_grp, NA_HEADS_PER_GROUP * GRID_W, KH * GRID_W),
                         lambda b, i: (layer, 0, 0, 0, 0), pipeline_mode=pl.Buffered(1)),
            pl.BlockSpec((None, 1, D_A), lambda b, i: (layer, 0, 0)),
        ],
        out_specs=pl.BlockSpec((blk, D_A), cur),
        out_shape=jax.ShapeDtypeStruct((batch * seq, D_A), BF16),
        compiler_params=_params(2),
        name="na_attention",
    )(q, k, v, bias, g)


def _swa_penalty_table():
    shape = (3, H_KV, GROUP, BLOCK, 3 * BLOCK)
    tq = lax.broadcasted_iota(jnp.int32, shape, 3)
    ts = lax.broadcasted_iota(jnp.int32, shape, 4)
    shift = lax.broadcasted_iota(jnp.int32, shape, 0) * BLOCK
    dist = jnp.abs(tq + shift - ts)
    slopes = jnp.exp2(-8.0 * jnp.arange(1, H_B + 1, dtype=F32) / H_B).reshape(1, H_KV, GROUP, 1, 1)
    pen = jnp.where(dist <= WINDOW, -(slopes * LOG2E) * dist.astype(F32), -jnp.inf)
    return pen.transpose(0, 2, 3, 1, 4).reshape(3, GROUP, BLOCK, H_KV * 3 * BLOCK)


def _swa_kernel(sink_ref, q_ref, kwin_ref, vwin_ref, pen_ref, g_ref, o_ref, k_bd, v_bd, *, nb):
    i = pl.program_id(1)
    nwin = SWA_BLOCKS + 2
    wlen = 3 * BLOCK
    win0 = jnp.clip(SWA_BLOCKS * i - 1, 0, nb - nwin)

    kw = kwin_ref[...]
    vw = vwin_ref[...]
    low = lax.broadcasted_iota(jnp.int32, (nwin * BLOCK, D_KV), 1) < HEAD_DIM
    zero = jnp.zeros((), BF16)
    k_bd[0] = jnp.where(low, kw, zero)
    k_bd[1] = jnp.where(low, zero, kw)
    ones_low = jnp.where(low, 1.0, 0.0).astype(BF16)
    ones_high = jnp.where(low, 0.0, 1.0).astype(BF16)
    v_bd[0] = jnp.concatenate([jnp.where(low, vw, zero), ones_low], axis=1)
    v_bd[1] = jnp.concatenate([jnp.where(low, zero, vw), ones_high], axis=1)

    low_q = lax.broadcasted_iota(jnp.int32, (BLOCK, D_KV), 1) < HEAD_DIM
    g = g_ref[...]
    for u in range(SWA_BLOCKS):
        qb = SWA_BLOCKS * i + u
        wb = jnp.clip(qb - 1, 0, nb - 3)
        off = pl.multiple_of((wb - win0) * BLOCK, BLOCK)
        place = qb - wb
        rows = slice(u * BLOCK, (u + 1) * BLOCK)
        q4 = jnp.concatenate([q_ref[rows, j * D_KV:(j + 1) * D_KV] for j in range(GROUP)], axis=0)
        k = jnp.concatenate([k_bd[0, pl.ds(off, wlen), :], k_bd[1, pl.ds(off, wlen), :]], axis=0)
        v = jnp.concatenate([v_bd[0, pl.ds(off, wlen), :], v_bd[1, pl.ds(off, wlen), :]], axis=0)
        s = lax.dot_general(q4, k, (((1,), (1,)), ((), ())), preferred_element_type=F32)
        es, sink_terms = [], []
        for j in range(GROUP):
            logits = s[j * BLOCK:(j + 1) * BLOCK] + pen_ref[place, j]
            halves = []
            for kh in range(H_KV):
                sink = sink_ref[H_KV * j + kh]
                lg = logits[:, kh * wlen:(kh + 1) * wlen]
                m = jnp.maximum(jnp.max(lg, axis=-1, keepdims=True), sink)
                es.append(jnp.exp2(lg - m).astype(BF16))
                halves.append(jnp.exp2(sink - m))
            sink_terms.append(jnp.where(low_q, halves[0], halves[1]))
        p = jnp.concatenate(
            [jnp.concatenate(es[H_KV * j:H_KV * (j + 1)], axis=1) for j in range(GROUP)], axis=0)
        c = jnp.dot(p, v, preferred_element_type=F32)
        outs = []
        for j in range(GROUP):
            cj = c[j * BLOCK:(j + 1) * BLOCK]
            outs.append(cj[:, :D_KV] * (1.0 / (cj[:, D_KV:] + sink_terms[j])))
        o_ref[rows, :] = _rms(jnp.concatenate(outs, axis=1), g).astype(BF16)


def _swa_attention(q, k, v, sinks, pen, g, layer, batch, seq):
    nb = seq // BLOCK
    nsteps = nb // SWA_BLOCKS
    nwin = SWA_BLOCKS + 2
    cur = lambda b, i: (b * nsteps + i, 0)
    win = lambda b, i: (pl.multiple_of(
        (b * nb + jnp.clip(SWA_BLOCKS * i - 1, 0, nb - nwin)) * BLOCK, BLOCK), 0)
    return pl.pallas_call(
        functools.partial(_swa_kernel, nb=nb),
        grid=(batch, nsteps),
        in_specs=[
            pl.BlockSpec(memory_space=pltpu.SMEM),
            pl.BlockSpec((SWA_BLOCKS * BLOCK, D_B), cur),
            pl.BlockSpec((pl.Element(nwin * BLOCK), pl.Element(D_KV)), win),
            pl.BlockSpec((pl.Element(nwin * BLOCK), pl.Element(D_KV)), win),
            pl.BlockSpec((3, GROUP, BLOCK, H_KV * 3 * BLOCK), lambda b, i: (0, 0, 0, 0),
                         pipeline_mode=pl.Buffered(1)),
            pl.BlockSpec((None, 1, D_B), lambda b, i: (layer, 0, 0)),
        ],
        out_specs=pl.BlockSpec((SWA_BLOCKS * BLOCK, D_B), cur),
        out_shape=jax.ShapeDtypeStruct((batch * seq, D_B), BF16),
        scratch_shapes=[
            pltpu.VMEM((H_KV, nwin * BLOCK, D_KV), BF16),
            pltpu.VMEM((H_KV, nwin * BLOCK, 2 * D_KV), BF16),
        ],
        compiler_params=_params(2),
        name="swa_attention",
    )(sinks, q, k, v, pen, g)


def _mix_ffn_kernel(xp_ref, x_ref, xn_ref, ap_ref, a_ref, an_ref, bp_ref, b_ref, bn_ref,
                    wo_ref, g_ref, wup_ref, cw_ref, cb_ref, wd_ref, gf_ref,
                    o_ref, ab_ext, xmid, h_ext, act, *, tiles_per_seq, final):
    i = pl.program_id(0)
    tm = x_ref.shape[0]
    ext = tm + 2 * HALO
    main = slice(HALO, HALO + tm)
    g = g_ref[...]
    t = i % tiles_per_seq
    keep_prev = (t > 0).astype(F32)
    keep_next = (t < tiles_per_seq - 1).astype(F32)

    ab_ext[0:HALO, 0:D_A] = ap_ref[...]
    ab_ext[0:HALO, D_A:] = bp_ref[...]
    ab_ext[main, 0:D_A] = a_ref[...]
    ab_ext[main, D_A:] = b_ref[...]
    ab_ext[HALO + tm:ext, 0:D_A] = an_ref[...]
    ab_ext[HALO + tm:ext, D_A:] = bn_ref[...]
    mix = jnp.dot(ab_ext[...], wo_ref[...], preferred_element_type=F32)
    x_mid = x_ref[...] + mix[main]
    xmid[...] = x_mid
    h_ext[0:HALO] = (_rms(xp_ref[...] + mix[0:HALO], g) * keep_prev).astype(BF16)
    h_ext[main] = _rms(x_mid, g).astype(BF16)
    h_ext[HALO + tm:ext] = (_rms(xn_ref[...] + mix[HALO + tm:ext], g) * keep_next).astype(BF16)

    def conv(u, col0):
        cols = slice(col0, col0 + FF_CHUNK)
        up = pltpu.roll(u, 1, 0)[main]
        un = pltpu.roll(u, ext - 1, 0)[main]
        return (up * cw_ref[0:1, cols] + u[main] * cw_ref[1:2, cols] + un * cw_ref[2:3, cols]
                + cb_ref[0:1, cols])

    for c in range(D_FF // FF_CHUNK):
        gc0 = c * FF_CHUNK
        vc0 = D_FF + c * FF_CHUNK
        hx = h_ext[...]
        ug = jnp.dot(hx, wup_ref[:, gc0:gc0 + FF_CHUNK], preferred_element_type=F32)
        uv = jnp.dot(hx, wup_ref[:, vc0:vc0 + FF_CHUNK], preferred_element_type=F32)
        gate = conv(ug, gc0)
        val = conv(uv, vc0)
        act[:, gc0:gc0 + FF_CHUNK] = (gate * (1.0 / (1.0 + jnp.exp(-gate))) * val).astype(BF16)

    y = xmid[...] + jnp.dot(act[...], wd_ref[...], preferred_element_type=F32)
    if final:
        y = _rms(y, gf_ref[...])
    o_ref[...] = y


def _mix_ffn(x, a, b, p, layer, seq, final):
    m = x.shape[0]
    tm = TM_FFN
    tiles_per_seq = seq // tm
    hb = tm // HALO
    n_halo_blocks = m // HALO
    prev = lambda i: (jnp.maximum(i * hb - 1, 0), 0)
    cur = lambda i: (i, 0)
    nxt = lambda i: (jnp.minimum((i + 1) * hb, n_halo_blocks - 1), 0)
    lay = lambda i: (layer, 0, 0)
    resident = dict(pipeline_mode=pl.Buffered(1))

    def halo3(width):
        return [pl.BlockSpec((HALO, width), prev), pl.BlockSpec((tm, width), cur),
                pl.BlockSpec((HALO, width), nxt)]

    return pl.pallas_call(
        functools.partial(_mix_ffn_kernel, tiles_per_seq=tiles_per_seq, final=final),
        grid=(m // tm,),
        in_specs=halo3(D_MODEL) + halo3(D_A) + halo3(D_B) + [
            pl.BlockSpec((None, D_MIX, D_MODEL), lay, **resident),
            pl.BlockSpec((None, 1, D_MODEL), lay),
            pl.BlockSpec((None, D_MODEL, 2 * D_FF), lay, **resident),
            pl.BlockSpec((None, 3, 2 * D_FF), lay),
            pl.BlockSpec((None, 1, 2 * D_FF), lay),
            pl.BlockSpec((None, D_FF, D_MODEL), lay, **resident),
            pl.BlockSpec((1, D_MODEL), lambda i: (0, 0)),
        ],
        out_specs=pl.BlockSpec((tm, D_MODEL), cur),
        out_shape=jax.ShapeDtypeStruct((m, D_MODEL), F32),
        scratch_shapes=[
            pltpu.VMEM((tm + 2 * HALO, D_MIX), BF16),
            pltpu.VMEM((tm, D_MODEL), F32),
            pltpu.VMEM((tm + 2 * HALO, D_MODEL), BF16),
            pltpu.VMEM((tm, D_FF), BF16),
        ],
        compiler_params=_params(1),
        name="mix_ffn",
    )(x, x, x, a, a, a, b, b, b, p["w_out"], p["norm_ffn"], p["w_up"], p["conv_w"],
      p["conv_b"], p["w_down"], p["norm_final"])


def _trunk(x, p):
    batch, seq, _ = x.shape
    depth = p["w_in"].shape[0]
    xf = x.reshape(batch * seq, D_MODEL)
    for l in range(depth):
        qa, ka, va, qb, kb, vb = _in_proj(xf, p["norm_mix"], p["w_in"], l)
        oa = _na_attention(qa, ka, va, p["na_bias"], p["norm_grp_a"], l, batch, seq)
        ob = _swa_attention(qb, kb, vb, p["sinks"][l], p["swa_pen"], p["norm_grp_b"], l,
                            batch, seq)
        xf = _mix_ffn(xf, oa, ob, p, l, seq, final=(l == depth - 1))
    return xf.reshape(batch, seq, D_MODEL)


def _interleave_kv_groups(a, axis, width):
    shape = a.shape
    a = a.reshape(shape[:axis] + (H_KV, GROUP, width) + shape[axis + 1:])
    return jnp.swapaxes(a, axis, axis + 1).reshape(shape)


def kernel(x_prompt, x_sample, norm_mix, w_in, rpb, sinks, norm_grp, w_out, norm_ffn, w_up,
           conv_w, conv_b, w_down, norm_final):
    depth = w_in.shape[0]
    qb0 = 3 * D_A
    w_in_perm = jnp.concatenate(
        [w_in[..., :qb0], _interleave_kv_groups(w_in[..., qb0:qb0 + D_B], 2, HEAD_DIM),
         w_in[..., qb0 + D_B:]], axis=-1)
    w_out_perm = jnp.concatenate(
        [w_out[:, :D_A], _interleave_kv_groups(w_out[:, D_A:], 1, HEAD_DIM)], axis=1)
    p = {
        "norm_mix": norm_mix.reshape(depth, 1, D_MODEL),
        "w_in": w_in_perm.astype(BF16),
        "na_bias": jnp.stack([_na_bias_table(rpb[l]) for l in range(depth)]),
        "sinks": _interleave_kv_groups(sinks.astype(F32) * LOG2E, 1, 1),
        "swa_pen": _swa_penalty_table(),
        "norm_grp_a": norm_grp[:, :D_A].reshape(depth, 1, D_A),
        "norm_grp_b": _interleave_kv_groups(norm_grp[:, D_A:], 1, HEAD_DIM).reshape(depth, 1, D_B),
        "w_out": w_out_perm.astype(BF16),
        "norm_ffn": norm_ffn.reshape(depth, 1, D_MODEL),
        "w_up": w_up.astype(BF16),
        "conv_w": conv_w,
        "conv_b": conv_b.reshape(depth, 1, 2 * D_FF),
        "w_down": w_down.astype(BF16),
        "norm_final": norm_final.reshape(1, D_MODEL),
    }
    return (_trunk(x_prompt, p), _trunk(x_sample, p))
```

```python
import functools
import math

import jax
import jax.numpy as jnp
import numpy as np
from jax import lax
from jax.experimental import pallas as pl
from jax.experimental.pallas import tpu as pltpu

D_MODEL = 1024
HEAD_DIM = 64
H_A = 8
D_A = H_A * HEAD_DIM
H_B = 8
H_KV = 2
GROUP = H_B // H_KV
D_B = H_B * HEAD_DIM
D_KV = H_KV * HEAD_DIM
D_MIX = D_A + D_B
W_IN_COLS = 3 * D_A + D_B + 2 * D_KV
GRID_W = 64
KH = 8
KW = 16
WINDOW = 128
BLOCK = 128
D_FF = 2816
EPS = 1e-6
SCALE = 1.0 / math.sqrt(HEAD_DIM)
LOG2E = math.log2(math.e)
Q_SCALE = SCALE * LOG2E

F32 = jnp.float32
BF16 = jnp.bfloat16

VMEM_LIMIT_BYTES = 56 * 1024 * 1024

TM_PROJ = 1024
TM_FFN = 1024
HALO = 16
FF_CHUNK = 256
NA_ROWS = 8
NA_HEADS_PER_GROUP = 4
SWA_BLOCKS = 16


def _rms(x, g):
    inv = lax.rsqrt(jnp.mean(x * x, axis=-1, keepdims=True) + EPS)
    return (x * inv) * g


def _params(n_axes):
    return pltpu.CompilerParams(
        dimension_semantics=("parallel",) * n_axes,
        vmem_limit_bytes=VMEM_LIMIT_BYTES,
    )


_SEGS = (
    (0, D_A, Q_SCALE),
    (D_A, D_A, None),
    (2 * D_A, D_A, None),
    (3 * D_A, D_B, Q_SCALE),
    (3 * D_A + D_B, D_KV, None),
    (3 * D_A + D_B + D_KV, D_KV, None),
)


def _in_proj_kernel(x_ref, g_ref, w_ref, *out_refs):
    h = _rms(x_ref[...], g_ref[...]).astype(BF16)
    for (start, width, scale), o_ref in zip(_SEGS, out_refs):
        p = jnp.dot(h, w_ref[:, start:start + width], preferred_element_type=F32)
        if scale is not None:
            p = p * scale
        o_ref[...] = p.astype(BF16)


def _in_proj(x, g, w, layer):
    m = x.shape[0]
    tm = TM_PROJ
    lay = lambda i: (layer, 0, 0)
    out_shape = tuple(jax.ShapeDtypeStruct((m, width), BF16) for _, width, _ in _SEGS)
    out_specs = tuple(pl.BlockSpec((tm, width), lambda i: (i, 0)) for _, width, _ in _SEGS)
    return pl.pallas_call(
        _in_proj_kernel,
        grid=(m // tm,),
        in_specs=[
            pl.BlockSpec((tm, D_MODEL), lambda i: (i, 0)),
            pl.BlockSpec((None, 1, D_MODEL), lay),
            pl.BlockSpec((None, D_MODEL, W_IN_COLS), lay),
        ],
        out_specs=out_specs,
        out_shape=out_shape,
        compiler_params=_params(1),
        name="in_proj",
    )(x, g, w)


def _na_bias_table(rpb):
    cols = np.arange(GRID_W)
    col_start = np.clip(cols - KW // 2, 0, GRID_W - KW)
    c2 = cols[None, :]
    valid = (c2 >= col_start[:, None]) & (c2 < col_start[:, None] + KW)
    dc = np.clip(c2 - cols[:, None] + (KW - 1), 0, 2 * KW - 2)
    onehot = jnp.asarray(dc[None] == np.arange(2 * KW - 1)[:, None, None], dtype=F32)
    picked = jnp.einsum("hek,kcd->hecd", rpb.astype(F32), onehot,
                        precision=lax.Precision.HIGHEST)
    e = jnp.where(jnp.asarray(valid)[None, None], picked * LOG2E, -jnp.inf)
    tabs = []
    for d in range(KH):
        t = e[:, KH - 1 - d:2 * KH - 1 - d]
        tabs.append(t.transpose(0, 2, 1, 3).reshape(
            H_A // NA_HEADS_PER_GROUP, NA_HEADS_PER_GROUP * GRID_W, KH * GRID_W))
    return jnp.stack(tabs)


def _na_kernel(q_ref, kwin, vwin, bias_ref, g_ref, o_ref, *, rows):
    i = pl.program_id(1)
    grp = NA_HEADS_PER_GROUP * HEAD_DIM
    win_row0 = jnp.clip((i - 1) * NA_ROWS, 0, rows - 3 * NA_ROWS)
    g = g_ref[...]
    head_of_row = lax.broadcasted_iota(jnp.int32, (grp, grp), 0) // HEAD_DIM
    head_of_lane = lax.broadcasted_iota(jnp.int32, (grp, grp), 1) // HEAD_DIM
    diag = head_of_row == head_of_lane
    lane_head = lax.broadcasted_iota(jnp.int32, (GRID_W, grp), 1) // HEAD_DIM

    def row_body(rr, carry):
        r = i * NA_ROWS + rr
        rs = jnp.clip(r - KH // 2, 0, rows - KH)
        d = r - rs
        off = pl.multiple_of((rs - win_row0) * GRID_W, GRID_W)
        qoff = pl.multiple_of(rr * GRID_W, GRID_W)
        outs = []
        for gi in range(H_A // NA_HEADS_PER_GROUP):
            lanes = slice(gi * grp, (gi + 1) * grp)
            q4 = q_ref[pl.ds(qoff, GRID_W), lanes]
            qbd = jnp.where(diag, jnp.concatenate([q4] * NA_HEADS_PER_GROUP, axis=0),
                            jnp.zeros((), BF16))
            k = kwin[pl.ds(off, KH * GRID_W), lanes]
            v = vwin[pl.ds(off, KH * GRID_W), lanes]
            s = lax.dot_general(qbd, k, (((1,), (1,)), ((), ())), preferred_element_type=F32)
            s = s + bias_ref[d, gi]
            m = jnp.max(s, axis=-1, keepdims=True)
            p = jnp.exp2(s - m)
            l = jnp.sum(p, axis=-1, keepdims=True)
            c = jnp.dot(p.astype(BF16), v, preferred_element_type=F32)
            c = c * (1.0 / l)
            out = c[0:GRID_W]
            for h in range(1, NA_HEADS_PER_GROUP):
                out = jnp.where(lane_head == h, c[h * GRID_W:(h + 1) * GRID_W], out)
            outs.append(out)
        o_ref[pl.ds(qoff, GRID_W), :] = _rms(jnp.concatenate(outs, axis=1), g).astype(BF16)
        return carry

    lax.fori_loop(0, NA_ROWS, row_body, 0, unroll=8)


def _na_attention(q, k, v, bias, g, layer, batch, seq):
    rows = seq // GRID_W
    nblk = rows // NA_ROWS
    blk = NA_ROWS * GRID_W
    cur = lambda b, i: (b * nblk + i, 0)
    win = lambda b, i: (pl.multiple_of((b * nblk + jnp.clip(i - 1, 0, nblk - 3)) * blk, blk), 0)
    n_grp = H_A // NA_HEADS_PER_GROUP
    return pl.pallas_call(
        functools.partial(_na_kernel, rows=rows),
        grid=(batch, nblk),
        in_specs=[
            pl.BlockSpec((blk, D_A), cur),
            pl.BlockSpec((pl.Element(3 * blk), pl.Element(D_A)), win),
            pl.BlockSpec((pl.Element(3 * blk), pl.Element(D_A)), win),
            pl.BlockSpec((None, KH, n_grp, NA_HEADS_PER_GROUP * GRID_W, KH * GRID_W),
                         lambda b, i: (layer, 0, 0, 0, 0), pipeline_mode=pl.Buffered(1)),
            pl.BlockSpec((None, 1, D_A), lambda b, i: (layer, 0, 0)),
        ],
        out_specs=pl.BlockSpec((blk, D_A), cur),
        out_shape=jax.ShapeDtypeStruct((batch * seq, D_A), BF16),
        compiler_params=_params(2),
        name="na_attention",
    )(q, k, v, bias, g)


def _swa_penalty_table():
    shape = (3, H_KV, GROUP, BLOCK, 3 * BLOCK)
    tq = lax.broadcasted_iota(jnp.int32, shape, 3)
    ts = lax.broadcasted_iota(jnp.int32, shape, 4)
    shift = lax.broadcasted_iota(jnp.int32, shape, 0) * BLOCK
    dist = jnp.abs(tq + shift - ts)
    slopes = jnp.exp2(-8.0 * jnp.arange(1, H_B + 1, dtype=F32) / H_B).reshape(1, H_KV, GROUP, 1, 1)
    pen = jnp.where(dist <= WINDOW, -(slopes * LOG2E) * dist.astype(F32), -jnp.inf)
    return pen.transpose(0, 2, 3, 1, 4).reshape(3, GROUP, BLOCK, H_KV * 3 * BLOCK)


def _swa_kernel(sink_ref, q_ref, kwin_ref, vwin_ref, pen_ref, g_ref, o_ref, k_bd, v_bd, *, nb):
    i = pl.program_id(1)
    nwin = SWA_BLOCKS + 2
    wlen = 3 * BLOCK
    win0 = jnp.clip(SWA_BLOCKS * i - 1, 0, nb - nwin)

    kw = kwin_ref[...]
    vw = vwin_ref[...]
    low = lax.broadcasted_iota(jnp.int32, (nwin * BLOCK, D_KV), 1) < HEAD_DIM
    zero = jnp.zeros((), BF16)
    k_bd[0] = jnp.where(low, kw, zero)
    k_bd[1] = jnp.where(low, zero, kw)
    ones_low = jnp.where(low, 1.0, 0.0).astype(BF16)
    ones_high = jnp.where(low, 0.0, 1.0).astype(BF16)
    v_bd[0] = jnp.concatenate([jnp.where(low, vw, zero), ones_low], axis=1)
    v_bd[1] = jnp.concatenate([jnp.where(low, zero, vw), ones_high], axis=1)

    low_q = lax.broadcasted_iota(jnp.int32, (BLOCK, D_KV), 1) < HEAD_DIM
    g = g_ref[...]
    for u in range(SWA_BLOCKS):
        qb = SWA_BLOCKS * i + u
        wb = jnp.clip(qb - 1, 0, nb - 3)
        off = pl.multiple_of((wb - win0) * BLOCK, BLOCK)
        place = qb - wb
        rows = slice(u * BLOCK, (u + 1) * BLOCK)
        q4 = jnp.concatenate([q_ref[rows, j * D_KV:(j + 1) * D_KV] for j in range(GROUP)], axis=0)
        k = jnp.concatenate([k_bd[0, pl.ds(off, wlen), :], k_bd[1, pl.ds(off, wlen), :]], axis=0)
        v = jnp.concatenate([v_bd[0, pl.ds(off, wlen), :], v_bd[1, pl.ds(off, wlen), :]], axis=0)
        s = lax.dot_general(q4, k, (((1,), (1,)), ((), ())), preferred_element_type=F32)
        es, sink_terms = [], []
        for j in range(GROUP):
            logits = s[j * BLOCK:(j + 1) * BLOCK] + pen_ref[place, j]
            halves = []
            for kh in range(H_KV):
                sink = sink_ref[H_KV * j + kh]
                lg = logits[:, kh * wlen:(kh + 1) * wlen]
                m = jnp.maximum(jnp.max(lg, axis=-1, keepdims=True), sink)
                es.append(jnp.exp2(lg - m).astype(BF16))
                halves.append(jnp.exp2(sink - m))
            sink_terms.append(jnp.where(low_q, halves[0], halves[1]))
        p = jnp.concatenate(
            [jnp.concatenate(es[H_KV * j:H_KV * (j + 1)], axis=1) for j in range(GROUP)], axis=0)
        c = jnp.dot(p, v, preferred_element_type=F32)
        outs = []
        for j in range(GROUP):
            cj = c[j * BLOCK:(j + 1) * BLOCK]
            outs.append(cj[:, :D_KV] * (1.0 / (cj[:, D_KV:] + sink_terms[j])))
        o_ref[rows, :] = _rms(jnp.concatenate(outs, axis=1), g).astype(BF16)


def _swa_attention(q, k, v, sinks, pen, g, layer, batch, seq):
    nb = seq // BLOCK
    nsteps = nb // SWA_BLOCKS
    nwin = SWA_BLOCKS + 2
    cur = lambda b, i: (b * nsteps + i, 0)
    win = lambda b, i: (pl.multiple_of(
        (b * nb + jnp.clip(SWA_BLOCKS * i - 1, 0, nb - nwin)) * BLOCK, BLOCK), 0)
    return pl.pallas_call(
        functools.partial(_swa_kernel, nb=nb),
        grid=(batch, nsteps),
        in_specs=[
            pl.BlockSpec(memory_space=pltpu.SMEM),
            pl.BlockSpec((SWA_BLOCKS * BLOCK, D_B), cur),
            pl.BlockSpec((pl.Element(nwin * BLOCK), pl.Element(D_KV)), win),
            pl.BlockSpec((pl.Element(nwin * BLOCK), pl.Element(D_KV)), win),
            pl.BlockSpec((3, GROUP, BLOCK, H_KV * 3 * BLOCK), lambda b, i: (0, 0, 0, 0),
                         pipeline_mode=pl.Buffered(1)),
            pl.BlockSpec((None, 1, D_B), lambda b, i: (layer, 0, 0)),
        ],
        out_specs=pl.BlockSpec((SWA_BLOCKS * BLOCK, D_B), cur),
        out_shape=jax.ShapeDtypeStruct((batch * seq, D_B), BF16),
        scratch_shapes=[
            pltpu.VMEM((H_KV, nwin * BLOCK, D_KV), BF16),
            pltpu.VMEM((H_KV, nwin * BLOCK, 2 * D_KV), BF16),
        ],
        compiler_params=_params(2),
        name="swa_attention",
    )(sinks, q, k, v, pen, g)


def _mix_ffn_kernel(xp_ref, x_ref, xn_ref, ap_ref, a_ref, an_ref, bp_ref, b_ref, bn_ref,
                    wo_ref, g_ref, wup_ref, cw_ref, cb_ref, wd_ref, gf_ref,
                    o_ref, ab_ext, xmid, h_ext, act, *, tiles_per_seq, final):
    i = pl.program_id(0)
    tm = x_ref.shape[0]
    ext = tm + 2 * HALO
    main = slice(HALO, HALO + tm)
    g = g_ref[...]
    t = i % tiles_per_seq
    keep_prev = (t > 0).astype(F32)
    keep_next = (t < tiles_per_seq - 1).astype(F32)

    ab_ext[0:HALO, 0:D_A] = ap_ref[...]
    ab_ext[0:HALO, D_A:] = bp_ref[...]
    ab_ext[main, 0:D_A] = a_ref[...]
    ab_ext[main, D_A:] = b_ref[...]
    ab_ext[HALO + tm:ext, 0:D_A] = an_ref[...]
    ab_ext[HALO + tm:ext, D_A:] = bn_ref[...]
    mix = jnp.dot(ab_ext[...], wo_ref[...], preferred_element_type=F32)
    x_mid = x_ref[...] + mix[main]
    xmid[...] = x_mid
    h_ext[0:HALO] = (_rms(xp_ref[...] + mix[0:HALO], g) * keep_prev).astype(BF16)
    h_ext[main] = _rms(x_mid, g).astype(BF16)
    h_ext[HALO + tm:ext] = (_rms(xn_ref[...] + mix[HALO + tm:ext], g) * keep_next).astype(BF16)

    for c in range(D_FF // FF_CHUNK):
        cols = slice(2 * c * FF_CHUNK, 2 * (c + 1) * FF_CHUNK)
        u = jnp.dot(h_ext[...], wup_ref[:, cols], preferred_element_type=F32)
        up = pltpu.roll(u, 1, 0)[main]
        un = pltpu.roll(u, ext - 1, 0)[main]
        y = (up * cw_ref[0:1, cols] + u[main] * cw_ref[1:2, cols] + un * cw_ref[2:3, cols]
             + cb_ref[0:1, cols])
        gate = y[:, :FF_CHUNK]
        val = y[:, FF_CHUNK:]
        act[:, c * FF_CHUNK:(c + 1) * FF_CHUNK] = (
            gate * (1.0 / (1.0 + jnp.exp(-gate))) * val).astype(BF16)

    y = xmid[...] + jnp.dot(act[...], wd_ref[...], preferred_element_type=F32)
    if final:
        y = _rms(y, gf_ref[...])
    o_ref[...] = y


def _mix_ffn(x, a, b, p, layer, seq, final):
    m = x.shape[0]
    tm = TM_FFN
    tiles_per_seq = seq // tm
    hb = tm // HALO
    n_halo_blocks = m // HALO
    prev = lambda i: (jnp.maximum(i * hb - 1, 0), 0)
    cur = lambda i: (i, 0)
    nxt = lambda i: (jnp.minimum((i + 1) * hb, n_halo_blocks - 1), 0)
    lay = lambda i: (layer, 0, 0)
    resident = dict(pipeline_mode=pl.Buffered(1))

    def halo3(width):
        return [pl.BlockSpec((HALO, width), prev), pl.BlockSpec((tm, width), cur),
                pl.BlockSpec((HALO, width), nxt)]

    return pl.pallas_call(
        functools.partial(_mix_ffn_kernel, tiles_per_seq=tiles_per_seq, final=final),
        grid=(m // tm,),
        in_specs=halo3(D_MODEL) + halo3(D_A) + halo3(D_B) + [
            pl.BlockSpec((None, D_MIX, D_MODEL), lay, **resident),
            pl.BlockSpec((None, 1, D_MODEL), lay),
            pl.BlockSpec((None, D_MODEL, 2 * D_FF), lay, **resident),
            pl.BlockSpec((None, 3, 2 * D_FF), lay),
            pl.BlockSpec((None, 1, 2 * D_FF), lay),
            pl.BlockSpec((None, D_FF, D_MODEL), lay, **resident),
            pl.BlockSpec((1, D_MODEL), lambda i: (0, 0)),
        ],
        out_specs=pl.BlockSpec((tm, D_MODEL), cur),
        out_shape=jax.ShapeDtypeStruct((m, D_MODEL), F32),
        scratch_shapes=[
            pltpu.VMEM((tm + 2 * HALO, D_MIX), BF16),
            pltpu.VMEM((tm, D_MODEL), F32),
            pltpu.VMEM((tm + 2 * HALO, D_MODEL), BF16),
            pltpu.VMEM((tm, D_FF), BF16),
        ],
        compiler_params=_params(1),
        name="mix_ffn",
    )(x, x, x, a, a, a, b, b, b, p["w_out"], p["norm_ffn"], p["w_up"], p["conv_w"],
      p["conv_b"], p["w_down"], p["norm_final"])


def _trunk(x, p):
    batch, seq, _ = x.shape
    depth = p["w_in"].shape[0]
    xf = x.reshape(batch * seq, D_MODEL)
    for l in range(depth):
        qa, ka, va, qb, kb, vb = _in_proj(xf, p["norm_mix"], p["w_in"], l)
        oa = _na_attention(qa, ka, va, p["na_bias"], p["norm_grp_a"], l, batch, seq)
        ob = _swa_attention(qb, kb, vb, p["sinks"][l], p["swa_pen"], p["norm_grp_b"], l,
                            batch, seq)
        xf = _mix_ffn(xf, oa, ob, p, l, seq, final=(l == depth - 1))
    return xf.reshape(batch, seq, D_MODEL)


def _interleave_kv_groups(a, axis, width):
    shape = a.shape
    a = a.reshape(shape[:axis] + (H_KV, GROUP, width) + shape[axis + 1:])
    return jnp.swapaxes(a, axis, axis + 1).reshape(shape)


def _interleave_gate_value(a):
    lead = a.shape[:-1]
    a = a.reshape(lead + (2, D_FF // FF_CHUNK, FF_CHUNK))
    return jnp.swapaxes(a, -3, -2).reshape(lead + (2 * D_FF,))


def kernel(x_prompt, x_sample, norm_mix, w_in, rpb, sinks, norm_grp, w_out, norm_ffn, w_up,
           conv_w, conv_b, w_down, norm_final):
    depth = w_in.shape[0]
    qb0 = 3 * D_A
    w_in = w_in.astype(BF16)
    w_out = w_out.astype(BF16)
    w_in_perm = jnp.concatenate(
        [w_in[..., :qb0], _interleave_kv_groups(w_in[..., qb0:qb0 + D_B], 2, HEAD_DIM),
         w_in[..., qb0 + D_B:]], axis=-1)
    w_out_perm = jnp.concatenate(
        [w_out[:, :D_A], _interleave_kv_groups(w_out[:, D_A:], 1, HEAD_DIM)], axis=1)
    p = {
        "norm_mix": norm_mix.reshape(depth, 1, D_MODEL),
        "w_in": w_in_perm,
        "na_bias": jnp.stack([_na_bias_table(rpb[l]) for l in range(depth)]),
        "sinks": _interleave_kv_groups(sinks.astype(F32) * LOG2E, 1, 1),
        "swa_pen": _swa_penalty_table(),
        "norm_grp_a": norm_grp[:, :D_A].reshape(depth, 1, D_A),
        "norm_grp_b": _interleave_kv_groups(norm_grp[:, D_A:], 1, HEAD_DIM).reshape(depth, 1, D_B),
        "w_out": w_out_perm,
        "norm_ffn": norm_ffn.reshape(depth, 1, D_MODEL),
        "w_up": _interleave_gate_value(w_up.astype(BF16)),
        "conv_w": _interleave_gate_value(conv_w),
        "conv_b": _interleave_gate_value(conv_b).reshape(depth, 1, 2 * D_FF),
        "w_down": w_down.astype(BF16),
        "norm_final": norm_final.reshape(1, D_MODEL),
    }
    return (_trunk(x_prompt, p), _trunk(x_sample, p))
```

```python
import functools
import math

import jax
import jax.numpy as jnp
import numpy as np
from jax import lax
from jax.experimental import pallas as pl
from jax.experimental.pallas import tpu as pltpu

D_MODEL = 1024
HEAD_DIM = 64
H_A = 8
D_A = H_A * HEAD_DIM
H_B = 8
H_KV = 2
GROUP = H_B // H_KV
D_B = H_B * HEAD_DIM
D_KV = H_KV * HEAD_DIM
D_MIX = D_A + D_B
W_IN_COLS = 3 * D_A + D_B + 2 * D_KV
GRID_W = 64
KH = 8
KW = 16
WINDOW = 128
BLOCK = 128
D_FF = 2816
EPS = 1e-6
SCALE = 1.0 / math.sqrt(HEAD_DIM)
LOG2E = math.log2(math.e)
Q_SCALE = SCALE * LOG2E

F32 = jnp.float32
BF16 = jnp.bfloat16

VMEM_LIMIT_BYTES = 56 * 1024 * 1024

TM_PROJ = 1024
TM_FFN = 1024
HALO = 16
FF_CHUNK = 256
NA_ROWS = 16
NA_HEADS_PER_GROUP = 4
SWA_BLOCKS = 16


def _rms(x, g):
    inv = lax.rsqrt(jnp.mean(x * x, axis=-1, keepdims=True) + EPS)
    return (x * inv) * g


def _params(n_axes):
    return pltpu.CompilerParams(
        dimension_semantics=("parallel",) * n_axes,
        vmem_limit_bytes=VMEM_LIMIT_BYTES,
    )


_SEGS = (
    (0, D_A, Q_SCALE),
    (D_A, D_A, None),
    (2 * D_A, D_A, None),
    (3 * D_A, D_B, Q_SCALE),
    (3 * D_A + D_B, D_KV, None),
    (3 * D_A + D_B + D_KV, D_KV, None),
)


def _in_proj_kernel(x_ref, g_ref, w_ref, *out_refs):
    h = _rms(x_ref[...], g_ref[...]).astype(BF16)
    for (start, width, scale), o_ref in zip(_SEGS, out_refs):
        p = jnp.dot(h, w_ref[:, start:start + width], preferred_element_type=F32)
        if scale is not None:
            p = p * scale
        o_ref[...] = p.astype(BF16)


def _in_proj(x, g, w, layer):
    m = x.shape[0]
    tm = TM_PROJ
    lay = lambda i: (layer, 0, 0)
    out_shape = tuple(jax.ShapeDtypeStruct((m, width), BF16) for _, width, _ in _SEGS)
    out_specs = tuple(pl.BlockSpec((tm, width), lambda i: (i, 0)) for _, width, _ in _SEGS)
    return pl.pallas_call(
        _in_proj_kernel,
        grid=(m // tm,),
        in_specs=[
            pl.BlockSpec((tm, D_MODEL), lambda i: (i, 0)),
            pl.BlockSpec((None, 1, D_MODEL), lay),
            pl.BlockSpec((None, D_MODEL, W_IN_COLS), lay),
        ],
        out_specs=out_specs,
        out_shape=out_shape,
        compiler_params=_params(1),
        name="in_proj",
    )(x, g, w)


def _na_bias_table(rpb):
    cols = np.arange(GRID_W)
    col_start = np.clip(cols - KW // 2, 0, GRID_W - KW)
    c2 = cols[None, :]
    valid = (c2 >= col_start[:, None]) & (c2 < col_start[:, None] + KW)
    dc = np.clip(c2 - cols[:, None] + (KW - 1), 0, 2 * KW - 2)
    onehot = jnp.asarray(dc[None] == np.arange(2 * KW - 1)[:, None, None], dtype=F32)
    picked = jnp.einsum("hek,kcd->hecd", rpb.astype(F32), onehot,
                        precision=lax.Precision.HIGHEST)
    e = jnp.where(jnp.asarray(valid)[None, None], picked * LOG2E, -jnp.inf)
    tabs = []
    for d in range(KH):
        t = e[:, KH - 1 - d:2 * KH - 1 - d]
        tabs.append(t.transpose(0, 2, 1, 3).reshape(
            H_A // NA_HEADS_PER_GROUP, NA_HEADS_PER_GROUP * GRID_W, KH * GRID_W))
    return jnp.stack(tabs)


def _na_kernel(q_ref, kwin, vwin, bias_ref, g_ref, o_ref, *, rows):
    i = pl.program_id(1)
    grp = NA_HEADS_PER_GROUP * HEAD_DIM
    win_row0 = jnp.clip((i - 1) * NA_ROWS, 0, rows - 3 * NA_ROWS)
    g = g_ref[...]
    head_of_row = lax.broadcasted_iota(jnp.int32, (grp, grp), 0) // HEAD_DIM
    head_of_lane = lax.broadcasted_iota(jnp.int32, (grp, grp), 1) // HEAD_DIM
    diag = head_of_row == head_of_lane
    lane_head = lax.broadcasted_iota(jnp.int32, (GRID_W, grp), 1) // HEAD_DIM

    def row_body(rr, carry):
        r = i * NA_ROWS + rr
        rs = jnp.clip(r - KH // 2, 0, rows - KH)
        d = r - rs
        off = pl.multiple_of((rs - win_row0) * GRID_W, GRID_W)
        qoff = pl.multiple_of(rr * GRID_W, GRID_W)
        outs = []
        for gi in range(H_A // NA_HEADS_PER_GROUP):
            lanes = slice(gi * grp, (gi + 1) * grp)
            q4 = q_ref[pl.ds(qoff, GRID_W), lanes]
            qbd = jnp.where(diag, jnp.concatenate([q4] * NA_HEADS_PER_GROUP, axis=0),
                            jnp.zeros((), BF16))
            k = kwin[pl.ds(off, KH * GRID_W), lanes]
            v = vwin[pl.ds(off, KH * GRID_W), lanes]
            s = lax.dot_general(qbd, k, (((1,), (1,)), ((), ())), preferred_element_type=F32)
            s = s + bias_ref[d, gi]
            m = jnp.max(s, axis=-1, keepdims=True)
            p = jnp.exp2(s - m)
            l = jnp.sum(p, axis=-1, keepdims=True)
            c = jnp.dot(p.astype(BF16), v, preferred_element_type=F32)
            c = c * (1.0 / l)
            out = c[0:GRID_W]
            for h in range(1, NA_HEADS_PER_GROUP):
                out = jnp.where(lane_head == h, c[h * GRID_W:(h + 1) * GRID_W], out)
            outs.append(out)
        o_ref[pl.ds(qoff, GRID_W), :] = _rms(jnp.concatenate(outs, axis=1), g).astype(BF16)
        return carry

    lax.fori_loop(0, NA_ROWS, row_body, 0, unroll=16)


def _na_attention(q, k, v, bias, g, layer, batch, seq):
    rows = seq // GRID_W
    nblk = rows // NA_ROWS
    blk = NA_ROWS * GRID_W
    cur = lambda b, i: (b * nblk + i, 0)
    win = lambda b, i: (pl.multiple_of((b * nblk + jnp.clip(i - 1, 0, nblk - 3)) * blk, blk), 0)
    n_grp = H_A // NA_HEADS_PER_GROUP
    return pl.pallas_call(
        functools.partial(_na_kernel, rows=rows),
        grid=(batch, nblk),
        in_specs=[
            pl.BlockSpec((blk, D_A), cur),
            pl.BlockSpec((pl.Element(3 * blk), pl.Element(D_A)), win),
            pl.BlockSpec((pl.Element(3 * blk), pl.Element(D_A)), win),
            pl.BlockSpec((None, KH, n_grp, NA_HEADS_PER_GROUP * GRID_W, KH * GRID_W),
                         lambda b, i: (layer, 0, 0, 0, 0), pipeline_mode=pl.Buffered(1)),
            pl.BlockSpec((None, 1, D_A), lambda b, i: (layer, 0, 0)),
        ],
        out_specs=pl.BlockSpec((blk, D_A), cur),
        out_shape=jax.ShapeDtypeStruct((batch * seq, D_A), BF16),
        compiler_params=_params(2),
        name="na_attention",
    )(q, k, v, bias, g)


def _swa_penalty_table():
    shape = (3, H_KV, GROUP, BLOCK, 3 * BLOCK)
    tq = lax.broadcasted_iota(jnp.int32, shape, 3)
    ts = lax.broadcasted_iota(jnp.int32, shape, 4)
    shift = lax.broadcasted_iota(jnp.int32, shape, 0) * BLOCK
    dist = jnp.abs(tq + shift - ts)
    slopes = jnp.exp2(-8.0 * jnp.arange(1, H_B + 1, dtype=F32) / H_B).reshape(1, H_KV, GROUP, 1, 1)
    pen = jnp.where(dist <= WINDOW, -(slopes * LOG2E) * dist.astype(F32), -jnp.inf)
    return pen.transpose(0, 2, 3, 1, 4).reshape(3, GROUP, BLOCK, H_KV * 3 * BLOCK)


def _swa_kernel(sink_ref, q_ref, kwin_ref, vwin_ref, pen_ref, g_ref, o_ref, k_bd, v_bd, *, nb):
    i = pl.program_id(1)
    nwin = SWA_BLOCKS + 2
    wlen = 3 * BLOCK
    win0 = jnp.clip(SWA_BLOCKS * i - 1, 0, nb - nwin)

    kw = kwin_ref[...]
    vw = vwin_ref[...]
    low = lax.broadcasted_iota(jnp.int32, (nwin * BLOCK, D_KV), 1) < HEAD_DIM
    zero = jnp.zeros((), BF16)
    k_bd[0] = jnp.where(low, kw, zero)
    k_bd[1] = jnp.where(low, zero, kw)
    ones_low = jnp.where(low, 1.0, 0.0).astype(BF16)
    ones_high = jnp.where(low, 0.0, 1.0).astype(BF16)
    v_bd[0] = jnp.concatenate([jnp.where(low, vw, zero), ones_low], axis=1)
    v_bd[1] = jnp.concatenate([jnp.where(low, zero, vw), ones_high], axis=1)

    low_q = lax.broadcasted_iota(jnp.int32, (BLOCK, D_KV), 1) < HEAD_DIM
    g = g_ref[...]
    for u in range(SWA_BLOCKS):
        qb = SWA_BLOCKS * i + u
        wb = jnp.clip(qb - 1, 0, nb - 3)
        off = pl.multiple_of((wb - win0) * BLOCK, BLOCK)
        place = qb - wb
        rows = slice(u * BLOCK, (u + 1) * BLOCK)
        q4 = jnp.concatenate([q_ref[rows, j * D_KV:(j + 1) * D_KV] for j in range(GROUP)], axis=0)
        k = jnp.concatenate([k_bd[0, pl.ds(off, wlen), :], k_bd[1, pl.ds(off, wlen), :]], axis=0)
        v = jnp.concatenate([v_bd[0, pl.ds(off, wlen), :], v_bd[1, pl.ds(off, wlen), :]], axis=0)
        s = lax.dot_general(q4, k, (((1,), (1,)), ((), ())), preferred_element_type=F32)
        es, sink_terms = [], []
        for j in range(GROUP):
            logits = s[j * BLOCK:(j + 1) * BLOCK] + pen_ref[place, j]
            halves = []
            for kh in range(H_KV):
                sink = sink_ref[H_KV * j + kh]
                lg = logits[:, kh * wlen:(kh + 1) * wlen]
                m = jnp.maximum(jnp.max(lg, axis=-1, keepdims=True), sink)
                es.append(jnp.exp2(lg - m).astype(BF16))
                halves.append(jnp.exp2(sink - m))
            sink_terms.append(jnp.where(low_q, halves[0], halves[1]))
        p = jnp.concatenate(
            [jnp.concatenate(es[H_KV * j:H_KV * (j + 1)], axis=1) for j in range(GROUP)], axis=0)
        c = jnp.dot(p, v, preferred_element_type=F32)
        outs = []
        for j in range(GROUP):
            cj = c[j * BLOCK:(j + 1) * BLOCK]
            outs.append(cj[:, :D_KV] * (1.0 / (cj[:, D_KV:] + sink_terms[j])))
        o_ref[rows, :] = _rms(jnp.concatenate(outs, axis=1), g).astype(BF16)


def _swa_attention(q, k, v, sinks, pen, g, layer, batch, seq):
    nb = seq // BLOCK
    nsteps = nb // SWA_BLOCKS
    nwin = SWA_BLOCKS + 2
    cur = lambda b, i: (b * nsteps + i, 0)
    win = lambda b, i: (pl.multiple_of(
        (b * nb + jnp.clip(SWA_BLOCKS * i - 1, 0, nb - nwin)) * BLOCK, BLOCK), 0)
    return pl.pallas_call(
        functools.partial(_swa_kernel, nb=nb),
        grid=(batch, nsteps),
        in_specs=[
            pl.BlockSpec(memory_space=pltpu.SMEM),
            pl.BlockSpec((SWA_BLOCKS * BLOCK, D_B), cur),
            pl.BlockSpec((pl.Element(nwin * BLOCK), pl.Element(D_KV)), win),
            pl.BlockSpec((pl.Element(nwin * BLOCK), pl.Element(D_KV)), win),
            pl.BlockSpec((3, GROUP, BLOCK, H_KV * 3 * BLOCK), lambda b, i: (0, 0, 0, 0),
                         pipeline_mode=pl.Buffered(1)),
            pl.BlockSpec((None, 1, D_B), lambda b, i: (layer, 0, 0)),
        ],
        out_specs=pl.BlockSpec((SWA_BLOCKS * BLOCK, D_B), cur),
        out_shape=jax.ShapeDtypeStruct((batch * seq, D_B), BF16),
        scratch_shapes=[
            pltpu.VMEM((H_KV, nwin * BLOCK, D_KV), BF16),
            pltpu.VMEM((H_KV, nwin * BLOCK, 2 * D_KV), BF16),
        ],
        compiler_params=_params(2),
        name="swa_attention",
    )(sinks, q, k, v, pen, g)


def _mix_ffn_kernel(xp_ref, x_ref, xn_ref, ap_ref, a_ref, an_ref, bp_ref, b_ref, bn_ref,
                    wo_ref, g_ref, wup_ref, cw_ref, cb_ref, wd_ref, gf_ref,
                    o_ref, ab_ext, xmid, h_ext, act, *, tiles_per_seq, final):
    i = pl.program_id(0)
    tm = x_ref.shape[0]
    ext = tm + 2 * HALO
    main = slice(HALO, HALO + tm)
    g = g_ref[...]
    t = i % tiles_per_seq
    keep_prev = (t > 0).astype(F32)
    keep_next = (t < tiles_per_seq - 1).astype(F32)

    ab_ext[0:HALO, 0:D_A] = ap_ref[...]
    ab_ext[0:HALO, D_A:] = bp_ref[...]
    ab_ext[main, 0:D_A] = a_ref[...]
    ab_ext[main, D_A:] = b_ref[...]
    ab_ext[HALO + tm:ext, 0:D_A] = an_ref[...]
    ab_ext[HALO + tm:ext, D_A:] = bn_ref[...]
    mix = jnp.dot(ab_ext[...], wo_ref[...], preferred_element_type=F32)
    x_mid = x_ref[...] + mix[main]
    xmid[...] = x_mid
    h_ext[0:HALO] = (_rms(xp_ref[...] + mix[0:HALO], g) * keep_prev).astype(BF16)
    h_ext[main] = _rms(x_mid, g).astype(BF16)
    h_ext[HALO + tm:ext] = (_rms(xn_ref[...] + mix[HALO + tm:ext], g) * keep_next).astype(BF16)

    def conv(u, col0):
        cols = slice(col0, col0 + FF_CHUNK)
        up = pltpu.roll(u, 1, 0)[main]
        un = pltpu.roll(u, ext - 1, 0)[main]
        return (up * cw_ref[0:1, cols] + u[main] * cw_ref[1:2, cols] + un * cw_ref[2:3, cols]
                + cb_ref[0:1, cols])

    for c in range(D_FF // FF_CHUNK):
        gc0 = c * FF_CHUNK
        vc0 = D_FF + c * FF_CHUNK
        hx = h_ext[...]
        ug = jnp.dot(hx, wup_ref[:, gc0:gc0 + FF_CHUNK], preferred_element_type=F32)
        uv = jnp.dot(hx, wup_ref[:, vc0:vc0 + FF_CHUNK], preferred_element_type=F32)
        gate = conv(ug, gc0)
        val = conv(uv, vc0)
        act[:, gc0:gc0 + FF_CHUNK] = (gate * (1.0 / (1.0 + jnp.exp(-gate))) * val).astype(BF16)

    y = xmid[...] + jnp.dot(act[...], wd_ref[...], preferred_element_type=F32)
    if final:
        y = _rms(y, gf_ref[...])
    o_ref[...] = y


def _mix_ffn(x, a, b, p, layer, seq, final):
    m = x.shape[0]
    tm = TM_FFN
    tiles_per_seq = seq // tm
    hb = tm // HALO
    n_halo_blocks = m // HALO
    prev = lambda i: (jnp.maximum(i * hb - 1, 0), 0)
    cur = lambda i: (i, 0)
    nxt = lambda i: (jnp.minimum((i + 1) * hb, n_halo_blocks - 1), 0)
    lay = lambda i: (layer, 0, 0)
    resident = dict(pipeline_mode=pl.Buffered(1))

    def halo3(width):
        return [pl.BlockSpec((HALO, width), prev), pl.BlockSpec((tm, width), cur),
                pl.BlockSpec((HALO, width), nxt)]

    return pl.pallas_call(
        functools.partial(_mix_ffn_kernel, tiles_per_seq=tiles_per_seq, final=final),
        grid=(m // tm,),
        in_specs=halo3(D_MODEL) + halo3(D_A) + halo3(D_B) + [
            pl.BlockSpec((None, D_MIX, D_MODEL), lay, **resident),
            pl.BlockSpec((None, 1, D_MODEL), lay),
            pl.BlockSpec((None, D_MODEL, 2 * D_FF), lay, **resident),
            pl.BlockSpec((None, 3, 2 * D_FF), lay),
            pl.BlockSpec((None, 1, 2 * D_FF), lay),
            pl.BlockSpec((None, D_FF, D_MODEL), lay, **resident),
            pl.BlockSpec((1, D_MODEL), lambda i: (0, 0)),
        ],
        out_specs=pl.BlockSpec((tm, D_MODEL), cur),
        out_shape=jax.ShapeDtypeStruct((m, D_MODEL), F32),
        scratch_shapes=[
            pltpu.VMEM((tm + 2 * HALO, D_MIX), BF16),
            pltpu.VMEM((tm, D_MODEL), F32),
            pltpu.VMEM((tm + 2 * HALO, D_MODEL), BF16),
            pltpu.VMEM((tm, D_FF), BF16),
        ],
        compiler_params=_params(1),
        name="mix_ffn",
    )(x, x, x, a, a, a, b, b, b, p["w_out"], p["norm_ffn"], p["w_up"], p["conv_w"],
      p["conv_b"], p["w_down"], p["norm_final"])


def _trunk(x, p):
    batch, seq, _ = x.shape
    depth = p["w_in"].shape[0]
    xf = x.reshape(batch * seq, D_MODEL)
    for l in range(depth):
        qa, ka, va, qb, kb, vb = _in_proj(xf, p["norm_mix"], p["w_in"], l)
        oa = _na_attention(qa, ka, va, p["na_bias"], p["norm_grp_a"], l, batch, seq)
        ob = _swa_attention(qb, kb, vb, p["sinks"][l], p["swa_pen"], p["norm_grp_b"], l,
                            batch, seq)
        xf = _mix_ffn(xf, oa, ob, p, l, seq, final=(l == depth - 1))
    return xf.reshape(batch, seq, D_MODEL)


def _interleave_kv_groups(a, axis, width):
    shape = a.shape
    a = a.reshape(shape[:axis] + (H_KV, GROUP, width) + shape[axis + 1:])
    return jnp.swapaxes(a, axis, axis + 1).reshape(shape)


def kernel(x_prompt, x_sample, norm_mix, w_in, rpb, sinks, norm_grp, w_out, norm_ffn, w_up,
           conv_w, conv_b, w_down, norm_final):
    depth = w_in.shape[0]
    qb0 = 3 * D_A
    w_in_perm = jnp.concatenate(
        [w_in[..., :qb0], _interleave_kv_groups(w_in[..., qb0:qb0 + D_B], 2, HEAD_DIM),
         w_in[..., qb0 + D_B:]], axis=-1)
    w_out_perm = jnp.concatenate(
        [w_out[:, :D_A], _interleave_kv_groups(w_out[:, D_A:], 1, HEAD_DIM)], axis=1)
    p = {
        "norm_mix": norm_mix.reshape(depth, 1, D_MODEL),
        "w_in": w_in_perm.astype(BF16),
        "na_bias": jnp.stack([_na_bias_table(rpb[l]) for l in range(depth)]),
        "sinks": _interleave_kv_groups(sinks.astype(F32) * LOG2E, 1, 1),
        "swa_pen": _swa_penalty_table(),
        "norm_grp_a": norm_grp[:, :D_A].reshape(depth, 1, D_A),
        "norm_grp_b": _interleave_kv_groups(norm_grp[:, D_A:], 1, HEAD_DIM).reshape(depth, 1, D_B),
        "w_out": w_out_perm.astype(BF16),
        "norm_ffn": norm_ffn.reshape(depth, 1, D_MODEL),
        "w_up": w_up.astype(BF16),
        "conv_w": conv_w,
        "conv_b": conv_b.reshape(depth, 1, 2 * D_FF),
        "w_down": w_down.astype(BF16),
        "norm_final": norm_final.reshape(1, D_MODEL),
    }
    return (_trunk(x_prompt, p), _trunk(x_sample, p))
```

```python
import functools
import math

import jax
import jax.numpy as jnp
import numpy as np
from jax import lax
from jax.experimental import pallas as pl
from jax.experimental.pallas import tpu as pltpu

D_MODEL = 1024
HEAD_DIM = 64
H_A = 8
D_A = H_A * HEAD_DIM
H_B = 8
H_KV = 2
GROUP = H_B // H_KV
D_B = H_B * HEAD_DIM
D_KV = H_KV * HEAD_DIM
D_MIX = D_A + D_B
W_IN_COLS = 3 * D_A + D_B + 2 * D_KV
GRID_W = 64
KH = 8
KW = 16
WINDOW = 128
BLOCK = 128
D_FF = 2816
EPS = 1e-6
SCALE = 1.0 / math.sqrt(HEAD_DIM)
LOG2E = math.log2(math.e)
Q_SCALE = SCALE * LOG2E

F32 = jnp.float32
BF16 = jnp.bfloat16

VMEM_LIMIT_BYTES = 56 * 1024 * 1024

TM_PROJ = 1024
TM_FFN = 1024
HALO = 16
FF_CHUNK = 256
NA_ROWS = 16
NA_HEADS_PER_GROUP = 4
SWA_BLOCKS = 16


def _rms(x, g):
    inv = lax.rsqrt(jnp.mean(x * x, axis=-1, keepdims=True) + EPS)
    return (x * inv) * g


def _params(n_axes):
    return pltpu.CompilerParams(
        dimension_semantics=("parallel",) * n_axes,
        vmem_limit_bytes=VMEM_LIMIT_BYTES,
    )


_SEGS = (
    (0, D_A, Q_SCALE),
    (D_A, D_A, None),
    (2 * D_A, D_A, None),
    (3 * D_A, D_B, Q_SCALE),
    (3 * D_A + D_B, D_KV, None),
    (3 * D_A + D_B + D_KV, D_KV, None),
)


def _in_proj_kernel(x_ref, g_ref, w_ref, *out_refs):
    h = _rms(x_ref[...], g_ref[...]).astype(BF16)
    for (start, width, scale), o_ref in zip(_SEGS, out_refs):
        p = jnp.dot(h, w_ref[:, start:start + width], preferred_element_type=F32)
        if scale is not None:
            p = p * scale
        o_ref[...] = p.astype(BF16)


def _in_proj(x, g, w, layer):
    m = x.shape[0]
    tm = TM_PROJ
    lay = lambda i: (layer, 0, 0)
    out_shape = tuple(jax.ShapeDtypeStruct((m, width), BF16) for _, width, _ in _SEGS)
    out_specs = tuple(pl.BlockSpec((tm, width), lambda i: (i, 0)) for _, width, _ in _SEGS)
    return pl.pallas_call(
        _in_proj_kernel,
        grid=(m // tm,),
        in_specs=[
            pl.BlockSpec((tm, D_MODEL), lambda i: (i, 0)),
            pl.BlockSpec((None, 1, D_MODEL), lay),
            pl.BlockSpec((None, D_MODEL, W_IN_COLS), lay),
        ],
        out_specs=out_specs,
        out_shape=out_shape,
        compiler_params=_params(1),
        name="in_proj",
    )(x, g, w)


def _na_bias_table(rpb):
    cols = np.arange(GRID_W)
    col_start = np.clip(cols - KW // 2, 0, GRID_W - KW)
    c2 = cols[None, :]
    valid = (c2 >= col_start[:, None]) & (c2 < col_start[:, None] + KW)
    dc = np.clip(c2 - cols[:, None] + (KW - 1), 0, 2 * KW - 2)
    onehot = jnp.asarray(dc[None] == np.arange(2 * KW - 1)[:, None, None], dtype=F32)
    picked = jnp.einsum("hek,kcd->hecd", rpb.astype(F32), onehot,
                        precision=lax.Precision.HIGHEST)
    e = jnp.where(jnp.asarray(valid)[None, None], picked * LOG2E, -jnp.inf)
    flat = e.transpose(0, 2, 1, 3).reshape(
        H_A // NA_HEADS_PER_GROUP, NA_HEADS_PER_GROUP * GRID_W, (2 * KH - 1) * GRID_W)
    return jnp.stack([flat[:, :, (KH - 1 - d) * GRID_W:(2 * KH - 1 - d) * GRID_W]
                      for d in range(KH)])


def _na_kernel(q_ref, kwin, vwin, bias_ref, g_ref, o_ref, *, rows):
    i = pl.program_id(1)
    grp = NA_HEADS_PER_GROUP * HEAD_DIM
    win_row0 = jnp.clip((i - 1) * NA_ROWS, 0, rows - 3 * NA_ROWS)
    g = g_ref[...]
    head_of_row = lax.broadcasted_iota(jnp.int32, (grp, grp), 0) // HEAD_DIM
    head_of_lane = lax.broadcasted_iota(jnp.int32, (grp, grp), 1) // HEAD_DIM
    diag = head_of_row == head_of_lane
    lane_head = lax.broadcasted_iota(jnp.int32, (GRID_W, grp), 1) // HEAD_DIM

    def row_body(rr, carry):
        r = i * NA_ROWS + rr
        rs = jnp.clip(r - KH // 2, 0, rows - KH)
        d = r - rs
        off = pl.multiple_of((rs - win_row0) * GRID_W, GRID_W)
        qoff = pl.multiple_of(rr * GRID_W, GRID_W)
        outs = []
        for gi in range(H_A // NA_HEADS_PER_GROUP):
            lanes = slice(gi * grp, (gi + 1) * grp)
            q4 = q_ref[pl.ds(qoff, GRID_W), lanes]
            qbd = jnp.where(diag, jnp.concatenate([q4] * NA_HEADS_PER_GROUP, axis=0),
                            jnp.zeros((), BF16))
            k = kwin[pl.ds(off, KH * GRID_W), lanes]
            v = vwin[pl.ds(off, KH * GRID_W), lanes]
            s = lax.dot_general(qbd, k, (((1,), (1,)), ((), ())), preferred_element_type=F32)
            s = s + bias_ref[d, gi]
            m = jnp.max(s, axis=-1, keepdims=True)
            p = jnp.exp2(s - m)
            l = jnp.sum(p, axis=-1, keepdims=True)
            c = jnp.dot(p.astype(BF16), v, preferred_element_type=F32)
            c = c * (1.0 / l)
            out = c[0:GRID_W]
            for h in range(1, NA_HEADS_PER_GROUP):
                out = jnp.where(lane_head == h, c[h * GRID_W:(h + 1) * GRID_W], out)
            outs.append(out)
        o_ref[pl.ds(qoff, GRID_W), :] = _rms(jnp.concatenate(outs, axis=1), g).astype(BF16)
        return carry

    lax.fori_loop(0, NA_ROWS, row_body, 0, unroll=16)


def _na_attention(q, k, v, bias, g, layer, batch, seq):
    rows = seq // GRID_W
    nblk = rows // NA_ROWS
    blk = NA_ROWS * GRID_W
    cur = lambda b, i: (b * nblk + i, 0)
    win = lambda b, i: (pl.multiple_of((b * nblk + jnp.clip(i - 1, 0, nblk - 3)) * blk, blk), 0)
    n_grp = H_A // NA_HEADS_PER_GROUP
    return pl.pallas_call(
        functools.partial(_na_kernel, rows=rows),
        grid=(batch, nblk),
        in_specs=[
            pl.BlockSpec((blk, D_A), cur),
            pl.BlockSpec((pl.Element(3 * blk), pl.Element(D_A)), win),
            pl.BlockSpec((pl.Element(3 * blk), pl.Element(D_A)), win),
            pl.BlockSpec((None, KH, n_grp, NA_HEADS_PER_GROUP * GRID_W, KH * GRID_W),
                         lambda b, i: (layer, 0, 0, 0, 0), pipeline_mode=pl.Buffered(1)),
            pl.BlockSpec((None, 1, D_A), lambda b, i: (layer, 0, 0)),
        ],
        out_specs=pl.BlockSpec((blk, D_A), cur),
        out_shape=jax.ShapeDtypeStruct((batch * seq, D_A), BF16),
        compiler_params=_params(2),
        name="na_attention",
    )(q, k, v, bias, g)


def _swa_penalty_table():
    shape = (3, H_KV, GROUP, BLOCK, 3 * BLOCK)
    tq = lax.broadcasted_iota(jnp.int32, shape, 3)
    ts = lax.broadcasted_iota(jnp.int32, shape, 4)
    shift = lax.broadcasted_iota(jnp.int32, shape, 0) * BLOCK
    dist = jnp.abs(tq + shift - ts)
    slopes = jnp.exp2(-8.0 * jnp.arange(1, H_B + 1, dtype=F32) / H_B).reshape(1, H_KV, GROUP, 1, 1)
    pen = jnp.where(dist <= WINDOW, -(slopes * LOG2E) * dist.astype(F32), -jnp.inf)
    return pen.transpose(0, 2, 3, 1, 4).reshape(3, GROUP, BLOCK, H_KV * 3 * BLOCK)


def _swa_kernel(sink_ref, q_ref, kwin_ref, vwin_ref, pen_ref, g_ref, o_ref, k_bd, v_bd, *, nb):
    i = pl.program_id(1)
    nwin = SWA_BLOCKS + 2
    wlen = 3 * BLOCK
    win0 = jnp.clip(SWA_BLOCKS * i - 1, 0, nb - nwin)

    kw = kwin_ref[...]
    vw = vwin_ref[...]
    low = lax.broadcasted_iota(jnp.int32, (nwin * BLOCK, D_KV), 1) < HEAD_DIM
    zero = jnp.zeros((), BF16)
    k_bd[0] = jnp.where(low, kw, zero)
    k_bd[1] = jnp.where(low, zero, kw)
    ones_low = jnp.where(low, 1.0, 0.0).astype(BF16)
    ones_high = jnp.where(low, 0.0, 1.0).astype(BF16)
    v_bd[0] = jnp.concatenate([jnp.where(low, vw, zero), ones_low], axis=1)
    v_bd[1] = jnp.concatenate([jnp.where(low, zero, vw), ones_high], axis=1)

    low_q = lax.broadcasted_iota(jnp.int32, (BLOCK, D_KV), 1) < HEAD_DIM
    g = g_ref[...]
    for u in range(SWA_BLOCKS):
        qb = SWA_BLOCKS * i + u
        wb = jnp.clip(qb - 1, 0, nb - 3)
        off = pl.multiple_of((wb - win0) * BLOCK, BLOCK)
        place = qb - wb
        rows = slice(u * BLOCK, (u + 1) * BLOCK)
        q4 = jnp.concatenate([q_ref[rows, j * D_KV:(j + 1) * D_KV] for j in range(GROUP)], axis=0)
        k = jnp.concatenate([k_bd[0, pl.ds(off, wlen), :], k_bd[1, pl.ds(off, wlen), :]], axis=0)
        v = jnp.concatenate([v_bd[0, pl.ds(off, wlen), :], v_bd[1, pl.ds(off, wlen), :]], axis=0)
        s = lax.dot_general(q4, k, (((1,), (1,)), ((), ())), preferred_element_type=F32)
        es, sink_terms = [], []
        for j in range(GROUP):
            logits = s[j * BLOCK:(j + 1) * BLOCK] + pen_ref[place, j]
            halves = []
            for kh in range(H_KV):
                sink = sink_ref[H_KV * j + kh]
                lg = logits[:, kh * wlen:(kh + 1) * wlen]
                m = jnp.maximum(jnp.max(lg, axis=-1, keepdims=True), sink)
                es.append(jnp.exp2(lg - m).astype(BF16))
                halves.append(jnp.exp2(sink - m))
            sink_terms.append(jnp.where(low_q, halves[0], halves[1]))
        p = jnp.concatenate(
            [jnp.concatenate(es[H_KV * j:H_KV * (j + 1)], axis=1) for j in range(GROUP)], axis=0)
        c = jnp.dot(p, v, preferred_element_type=F32)
        outs = []
        for j in range(GROUP):
            cj = c[j * BLOCK:(j + 1) * BLOCK]
            outs.append(cj[:, :D_KV] * (1.0 / (cj[:, D_KV:] + sink_terms[j])))
        o_ref[rows, :] = _rms(jnp.concatenate(outs, axis=1), g).astype(BF16)


def _swa_attention(q, k, v, sinks, pen, g, layer, batch, seq):
    nb = seq // BLOCK
    nsteps = nb // SWA_BLOCKS
    nwin = SWA_BLOCKS + 2
    cur = lambda b, i: (b * nsteps + i, 0)
    win = lambda b, i: (pl.multiple_of(
        (b * nb + jnp.clip(SWA_BLOCKS * i - 1, 0, nb - nwin)) * BLOCK, BLOCK), 0)
    return pl.pallas_call(
        functools.partial(_swa_kernel, nb=nb),
        grid=(batch, nsteps),
        in_specs=[
            pl.BlockSpec(memory_space=pltpu.SMEM),
            pl.BlockSpec((SWA_BLOCKS * BLOCK, D_B), cur),
            pl.BlockSpec((pl.Element(nwin * BLOCK), pl.Element(D_KV)), win),
            pl.BlockSpec((pl.Element(nwin * BLOCK), pl.Element(D_KV)), win),
            pl.BlockSpec((3, GROUP, BLOCK, H_KV * 3 * BLOCK), lambda b, i: (0, 0, 0, 0),
                         pipeline_mode=pl.Buffered(1)),
            pl.BlockSpec((None, 1, D_B), lambda b, i: (layer, 0, 0)),
        ],
        out_specs=pl.BlockSpec((SWA_BLOCKS * BLOCK, D_B), cur),
        out_shape=jax.ShapeDtypeStruct((batch * seq, D_B), BF16),
        scratch_shapes=[
            pltpu.VMEM((H_KV, nwin * BLOCK, D_KV), BF16),
            pltpu.VMEM((H_KV, nwin * BLOCK, 2 * D_KV), BF16),
        ],
        compiler_params=_params(2),
        name="swa_attention",
    )(sinks, q, k, v, pen, g)


def _mix_ffn_kernel(xp_ref, x_ref, xn_ref, ap_ref, a_ref, an_ref, bp_ref, b_ref, bn_ref,
                    wo_ref, g_ref, wup_ref, cw_ref, cb_ref, wd_ref, gf_ref,
                    o_ref, ab_ext, xmid, h_ext, act, *, tiles_per_seq, final):
    i = pl.program_id(0)
    tm = x_ref.shape[0]
    ext = tm + 2 * HALO
    main = slice(HALO, HALO + tm)
    g = g_ref[...]
    t = i % tiles_per_seq
    keep_prev = (t > 0).astype(F32)
    keep_next = (t < tiles_per_seq - 1).astype(F32)

    ab_ext[0:HALO, 0:D_A] = ap_ref[...]
    ab_ext[0:HALO, D_A:] = bp_ref[...]
    ab_ext[main, 0:D_A] = a_ref[...]
    ab_ext[main, D_A:] = b_ref[...]
    ab_ext[HALO + tm:ext, 0:D_A] = an_ref[...]
    ab_ext[HALO + tm:ext, D_A:] = bn_ref[...]
    mix = jnp.dot(ab_ext[...], wo_ref[...], preferred_element_type=F32)
    x_mid = x_ref[...] + mix[main]
    xmid[...] = x_mid
    h_ext[0:HALO] = (_rms(xp_ref[...] + mix[0:HALO], g) * keep_prev).astype(BF16)
    h_ext[main] = _rms(x_mid, g).astype(BF16)
    h_ext[HALO + tm:ext] = (_rms(xn_ref[...] + mix[HALO + tm:ext], g) * keep_next).astype(BF16)

    def conv(u, col0):
        cols = slice(col0, col0 + FF_CHUNK)
        up = pltpu.roll(u, 1, 0)[main]
        un = pltpu.roll(u, ext - 1, 0)[main]
        return (up * cw_ref[0:1, cols] + u[main] * cw_ref[1:2, cols] + un * cw_ref[2:3, cols]
                + cb_ref[0:1, cols])

    for c in range(D_FF // FF_CHUNK):
        gc0 = c * FF_CHUNK
        vc0 = D_FF + c * FF_CHUNK
        hx = h_ext[...]
        ug = jnp.dot(hx, wup_ref[:, gc0:gc0 + FF_CHUNK], preferred_element_type=F32)
        uv = jnp.dot(hx, wup_ref[:, vc0:vc0 + FF_CHUNK], preferred_element_type=F32)
        gate = conv(ug, gc0)
        val = conv(uv, vc0)
        act[:, gc0:gc0 + FF_CHUNK] = (gate * (1.0 / (1.0 + jnp.exp(-gate))) * val).astype(BF16)

    y = xmid[...] + jnp.dot(act[...], wd_ref[...], preferred_element_type=F32)
    if final:
        y = _rms(y, gf_ref[...])
    o_ref[...] = y


def _mix_ffn(x, a, b, p, layer, seq, final):
    m = x.shape[0]
    tm = TM_FFN
    tiles_per_seq = seq // tm
    hb = tm // HALO
    n_halo_blocks = m // HALO
    prev = lambda i: (jnp.maximum(i * hb - 1, 0), 0)
    cur = lambda i: (i, 0)
    nxt = lambda i: (jnp.minimum((i + 1) * hb, n_halo_blocks - 1), 0)
    lay = lambda i: (layer, 0, 0)
    resident = dict(pipeline_mode=pl.Buffered(1))

    def halo3(width):
        return [pl.BlockSpec((HALO, width), prev), pl.BlockSpec((tm, width), cur),
                pl.BlockSpec((HALO, width), nxt)]

    return pl.pallas_call(
        functools.partial(_mix_ffn_kernel, tiles_per_seq=tiles_per_seq, final=final),
        grid=(m // tm,),
        in_specs=halo3(D_MODEL) + halo3(D_A) + halo3(D_B) + [
            pl.BlockSpec((None, D_MIX, D_MODEL), lay, **resident),
            pl.BlockSpec((None, 1, D_MODEL), lay),
            pl.BlockSpec((None, D_MODEL, 2 * D_FF), lay, **resident),
            pl.BlockSpec((None, 3, 2 * D_FF), lay),
            pl.BlockSpec((None, 1, 2 * D_FF), lay),
            pl.BlockSpec((None, D_FF, D_MODEL), lay, **resident),
            pl.BlockSpec((1, D_MODEL), lambda i: (0, 0)),
        ],
        out_specs=pl.BlockSpec((tm, D_MODEL), cur),
        out_shape=jax.ShapeDtypeStruct((m, D_MODEL), F32),
        scratch_shapes=[
            pltpu.VMEM((tm + 2 * HALO, D_MIX), BF16),
            pltpu.VMEM((tm, D_MODEL), F32),
            pltpu.VMEM((tm + 2 * HALO, D_MODEL), BF16),
            pltpu.VMEM((tm, D_FF), BF16),
        ],
        compiler_params=_params(1),
        name="mix_ffn",
    )(x, x, x, a, a, a, b, b, b, p["w_out"], p["norm_ffn"], p["w_up"], p["conv_w"],
      p["conv_b"], p["w_down"], p["norm_final"])


def _trunk(x, p):
    batch, seq, _ = x.shape
    depth = p["w_in"].shape[0]
    xf = x.reshape(batch * seq, D_MODEL)
    for l in range(depth):
        qa, ka, va, qb, kb, vb = _in_proj(xf, p["norm_mix"], p["w_in"], l)
        oa = _na_attention(qa, ka, va, p["na_bias"], p["norm_grp_a"], l, batch, seq)
        ob = _swa_attention(qb, kb, vb, p["sinks"][l], p["swa_pen"], p["norm_grp_b"], l,
                            batch, seq)
        xf = _mix_ffn(xf, oa, ob, p, l, seq, final=(l == depth - 1))
    return xf.reshape(batch, seq, D_MODEL)


def _interleave_kv_groups(a, axis, width):
    shape = a.shape
    a = a.reshape(shape[:axis] + (H_KV, GROUP, width) + shape[axis + 1:])
    return jnp.swapaxes(a, axis, axis + 1).reshape(shape)


def kernel(x_prompt, x_sample, norm_mix, w_in, rpb, sinks, norm_grp, w_out, norm_ffn, w_up,
           conv_w, conv_b, w_down, norm_final):
    depth = w_in.shape[0]
    qb0 = 3 * D_A
    w_in_perm = jnp.concatenate(
        [w_in[..., :qb0], _interleave_kv_groups(w_in[..., qb0:qb0 + D_B], 2, HEAD_DIM),
         w_in[..., qb0 + D_B:]], axis=-1)
    w_out_perm = jnp.concatenate(
        [w_out[:, :D_A], _interleave_kv_groups(w_out[:, D_A:], 1, HEAD_DIM)], axis=1)
    p = {
        "norm_mix": norm_mix.reshape(depth, 1, D_MODEL),
        "w_in": w_in_perm.astype(BF16),
        "na_bias": jnp.stack([_na_bias_table(rpb[l]) for l in range(depth)]),
        "sinks": _interleave_kv_groups(sinks.astype(F32) * LOG2E, 1, 1),
        "swa_pen": _swa_penalty_table(),
        "norm_grp_a": norm_grp[:, :D_A].reshape(depth, 1, D_A),
        "norm_grp_b": _interleave_kv_groups(norm_grp[:, D_A:], 1, HEAD_DIM).reshape(depth, 1, D_B),
        "w_out": w_out_perm.astype(BF16),
        "norm_ffn": norm_ffn.reshape(depth, 1, D_MODEL),
        "w_up": w_up.astype(BF16),
        "conv_w": conv_w,
        "conv_b": conv_b.reshape(depth, 1, 2 * D_FF),
        "w_down": w_down.astype(BF16),
        "norm_final": norm_final.reshape(1, D_MODEL),
    }
    return (_trunk(x_prompt, p), _trunk(x_sample, p))
```
